```python
import math
import jax
import jax.numpy as jnp
from jax import lax
import numpy as np

D_MODEL = 4096
BATCH = 2
SEQ = 4096
DEPTH = 2

CTX_LEN = 256
GRID_W = 64
Q_BLOCK = 128
ROPE_BASE = 10000.0
EPS = 1e-6

MLA_HEADS = 16
MLA_Q_RANK = 1024
MLA_KV_RANK = 512
MLA_NOPE = 128
MLA_ROPE = 64
MLA_V = 128

S5_GROUP = 16
S5_GROUPS = 64
S5_WIDTH = S5_GROUP * S5_GROUPS
S5_STATE = 64

DIFF_HEADS = 16
DIFF_QK = 64
DIFF_V = 2 * DIFF_QK

CONV_CH = 1024
CONV_W = 31

D_FF = -(-(8 * D_MODEL) // (3 * 256)) * 256
N_MOD = 6

EVEN_IN = MLA_Q_RANK + MLA_KV_RANK + MLA_ROPE + S5_WIDTH
EVEN_OUT = MLA_HEADS * MLA_V + S5_WIDTH
ODD_QK = DIFF_HEADS * 2 * DIFF_QK
ODD_V = DIFF_HEADS * DIFF_V
ODD_IN = 2 * ODD_QK + ODD_V + 2 * CONV_CH
ODD_OUT = ODD_V + CONV_CH
N_EVEN = (DEPTH + 1) // 2
N_ODD = DEPTH // 2

kernel_name = 'hybrid_s5_mla_diffattn_conformer_dit'


def rmsnorm(x, g):
    xf = x.astype(jnp.float32)
    xf = xf * lax.rsqrt(jnp.mean(xf * xf, axis=-1, keepdims=True) + EPS)
    return (xf * g.astype(jnp.float32)).astype(x.dtype)


def layernorm(x, g, b):
    xf = x.astype(jnp.float32)
    xc = xf - jnp.mean(xf, axis=-1, keepdims=True)
    xf = xc * lax.rsqrt(jnp.mean(xc * xc, axis=-1, keepdims=True) + EPS)
    return (xf * g.astype(jnp.float32) + b.astype(jnp.float32)).astype(x.dtype)


def swiglu(h, w1, w3, w2):
    return (jax.nn.silu(h @ w1) * (h @ w3)) @ w2


def axial_rope_tables(n_tok, dim):
    rows = n_tok // GRID_W
    row = jnp.broadcast_to(jnp.arange(rows, dtype=jnp.float32)[:, None], (rows, GRID_W)).reshape(-1)
    col = jnp.broadcast_to(jnp.arange(GRID_W, dtype=jnp.float32)[None, :], (rows, GRID_W)).reshape(-1)
    axis_dim = dim // 2
    inv = ROPE_BASE ** (-jnp.arange(0, axis_dim, 2, dtype=jnp.float32) / axis_dim)
    ang_r = row[:, None] * inv[None, :]
    ang_c = col[:, None] * inv[None, :]
    ang = jnp.concatenate([ang_r, ang_r, ang_c, ang_c], axis=-1)
    return jnp.cos(ang), jnp.sin(ang)


def apply_rope(x, cos, sin):
    shape = (1, cos.shape[0]) + (1,) * (x.ndim - 3) + (cos.shape[1],)
    r1, r2, c1, c2 = jnp.split(x, 4, axis=-1)
    rot = jnp.concatenate([-r2, r1, -c2, c1], axis=-1)
    return (x * cos.reshape(shape) + rot * sin.reshape(shape)).astype(x.dtype)


def sweep_query_blocks(fn, *qs):
    b, n = qs[0].shape[:2]
    nb = n // Q_BLOCK
    blocks = tuple(jnp.moveaxis(q.reshape((b, nb, Q_BLOCK) + q.shape[2:]), 1, 0) for q in qs)
    out = lax.map(lambda blk: fn(*blk), blocks)
    return jnp.moveaxis(out, 0, 1).reshape((b, n) + out.shape[3:])


def softmax_attend(q, k, v, scale):
    s = jnp.einsum('bqhd,bkhd->bhqk', q, k, preferred_element_type=jnp.float32) * scale
    p = jax.nn.softmax(s, axis=-1)
    return jnp.einsum('bhqk,bkhd->bqhd', p.astype(v.dtype), v)


def mla_project(cq, ckv, kr, g_q, w_uq, g_kv, w_ukv, cos, sin):
    b, n = cq.shape[:2]
    q = (rmsnorm(cq, g_q) @ w_uq).reshape(b, n, MLA_HEADS, MLA_NOPE + MLA_ROPE)
    kv = (rmsnorm(ckv, g_kv) @ w_ukv).reshape(b, n, MLA_HEADS, MLA_NOPE + MLA_V)
    q_nope, q_rope = jnp.split(q, [MLA_NOPE], axis=-1)
    k_nope, v = jnp.split(kv, [MLA_NOPE], axis=-1)
    k_rope = kr[:, :, None, :]
    if cos is not None:
        q_rope = apply_rope(q_rope, cos, sin)
        k_rope = apply_rope(k_rope, cos, sin)
    q = jnp.concatenate([q_nope, q_rope], axis=-1)
    k = jnp.concatenate([k_nope, jnp.broadcast_to(k_rope, (b, n, MLA_HEADS, MLA_ROPE))], axis=-1)
    return q, k, v


def cmul(ar, ai, br, bi):
    return ar * br - ai * bi, ar * bi + ai * br


def s5_discretise(lam_re, lam_im, log_dt, b_re, b_im):
    lr = lam_re.astype(jnp.float32)
    li = lam_im.astype(jnp.float32)
    dt = jnp.exp(log_dt.astype(jnp.float32))[:, None]
    mag = jnp.exp(lr * dt)
    ab_re = mag * jnp.cos(li * dt)
    ab_im = mag * jnp.sin(li * dt)
    den = lr * lr + li * li
    num_re = ab_re - 1.0
    coef_re = (num_re * lr + ab_im * li) / den
    coef_im = (ab_im * lr - num_re * li) / den
    bb_re, bb_im = cmul(coef_re[..., None], coef_im[..., None],
                        b_re.astype(jnp.float32), b_im.astype(jnp.float32))
    return ab_re, ab_im, bb_re, bb_im


def linear_scan(ab_re, ab_im, bu_re, bu_im, h0):
    if h0 is not None:
        dr, di = cmul(ab_re, ab_im, h0[0], h0[1])
        bu_re = bu_re.at[:, 0].add(dr)
        bu_im = bu_im.at[:, 0].add(di)
    a_re = jnp.broadcast_to(ab_re, bu_re.shape)
    a_im = jnp.broadcast_to(ab_im, bu_im.shape)

    def combine(e1, e2):
        a1r, a1i, b1r, b1i = e1
        a2r, a2i, b2r, b2i = e2
        ar, ai = cmul(a2r, a2i, a1r, a1i)
        br, bi = cmul(a2r, a2i, b1r, b1i)
        return ar, ai, br + b2r, bi + b2i

    _, _, h_re, h_im = lax.associative_scan(combine, (a_re, a_im, bu_re, bu_im), axis=1)
    return h_re, h_im


def s5_direction(u_ctx, u_lat, lam_re, lam_im, log_dt, b_re, b_im, c_re, c_im, reverse, need_ctx):
    ab_re, ab_im, bb_re, bb_im = s5_discretise(lam_re, lam_im, log_dt, b_re, b_im)
    cr = c_re.astype(jnp.float32)
    ci = c_im.astype(jnp.float32)

    def drive(u):
        u = jnp.flip(u, axis=1) if reverse else u
        return (jnp.einsum('bngi,gpi->bngp', u, bb_re), jnp.einsum('bngi,gpi->bngp', u, bb_im))

    def readout(h_re, h_im):
        y = jnp.einsum('gip,bngp->bngi', cr, h_re) - jnp.einsum('gip,bngp->bngi', ci, h_im)
        return jnp.flip(y, axis=1) if reverse else y

    hc_re, hc_im = linear_scan(ab_re, ab_im, *drive(u_ctx), None)
    hl_re, hl_im = linear_scan(ab_re, ab_im, *drive(u_lat), (hc_re[:, -1], hc_im[:, -1]))
    y_ctx = readout(hc_re, hc_im) if need_ctx else None
    return y_ctx, readout(hl_re, hl_im)


def s5_mixer(u_ctx, u_lat, lam_re, lam_im, log_dt, b_re, b_im, c_re, c_im, d_skip, w_glu, need_ctx):
    dtype = u_lat.dtype

    def groups(u):
        return u.astype(jnp.float32).reshape(u.shape[:2] + (S5_GROUPS, S5_GROUP))

    uc, ul = groups(u_ctx), groups(u_lat)
    d = d_skip.astype(jnp.float32).reshape(S5_GROUPS, S5_GROUP)
    yf_c, yf_l = s5_direction(uc, ul, lam_re[0], lam_im[0], log_dt[0], b_re[0], b_im[0],
                              c_re[0], c_im[0], False, need_ctx)
    yb_c, yb_l = s5_direction(uc, ul, lam_re[1], lam_im[1], log_dt[1], b_re[1], b_im[1],
                              c_re[1], c_im[1], True, need_ctx)

    def glu(y, u):
        y = jax.nn.gelu(y + d * u).reshape(u.shape[:2] + (S5_WIDTH,)).astype(dtype)
        ya, yb = jnp.split(y @ w_glu, 2, axis=-1)
        return ya * jax.nn.sigmoid(yb)

    out_lat = glu(yf_l + yb_l, ul)
    out_ctx = glu(yf_c + yb_c, uc) if need_ctx else None
    return out_ctx, out_lat


def even_mixer(h_ctx, h_lat, cos, sin, w_in, g_q, w_uq, g_kv, w_ukv, lam_re, lam_im, log_dt,
               b_re, b_im, c_re, c_im, d_skip, w_glu, w_out, need_ctx):
    cuts = [MLA_Q_RANK, MLA_Q_RANK + MLA_KV_RANK, MLA_Q_RANK + MLA_KV_RANK + MLA_ROPE]
    cq_c, ckv_c, kr_c, u_c = jnp.split(h_ctx @ w_in, cuts, axis=-1)
    cq_l, ckv_l, kr_l, u_l = jnp.split(h_lat @ w_in, cuts, axis=-1)
    q_c, k_c, v_c = mla_project(cq_c, ckv_c, kr_c, g_q, w_uq, g_kv, w_ukv, None, None)
    q_l, k_l, v_l = mla_project(cq_l, ckv_l, kr_l, g_q, w_uq, g_kv, w_ukv, cos, sin)
    k_all = jnp.concatenate([k_c, k_l], axis=1)
    v_all = jnp.concatenate([v_c, v_l], axis=1)
    scale = (MLA_NOPE + MLA_ROPE) ** -0.5
    a_l = sweep_query_blocks(lambda qb: softmax_attend(qb, k_all, v_all, scale), q_l)
    s_c, s_l = s5_mixer(u_c, u_l, lam_re, lam_im, log_dt, b_re, b_im, c_re, c_im, d_skip, w_glu, need_ctx)
    y_l = jnp.concatenate([a_l.reshape(a_l.shape[:2] + (MLA_HEADS * MLA_V,)), s_l], axis=-1) @ w_out
    y_c = None
    if need_ctx:
        a_c = softmax_attend(q_c, k_c, v_c, scale)
        y_c = jnp.concatenate([a_c.reshape(a_c.shape[:2] + (MLA_HEADS * MLA_V,)), s_c], axis=-1) @ w_out
    return y_c, y_l


def diff_attend(q1, q2, k1, k2, v, lam):
    scale = DIFF_QK ** -0.5
    s1 = jnp.einsum('bqhd,bkhd->bhqk', q1, k1, preferred_element_type=jnp.float32) * scale
    s2 = jnp.einsum('bqhd,bkhd->bhqk', q2, k2, preferred_element_type=jnp.float32) * scale
    p = jax.nn.softmax(s1, axis=-1) - lam * jax.nn.softmax(s2, axis=-1)
    return jnp.einsum('bhqk,bkhd->bqhd', p.astype(v.dtype), v)


def depthwise_conv(u, w, b):
    y = lax.conv_general_dilated(u, w[:, None, :].astype(u.dtype), window_strides=(1,),
                                 padding=[(CONV_W // 2, CONV_W // 2)],
                                 dimension_numbers=('NWC', 'WIO', 'NWC'),
                                 feature_group_count=u.shape[-1])
    return y + b


def conformer_conv(p, w_dw, b_dw, g_n, b_n):
    a, g = jnp.split(p, 2, axis=-1)
    y = depthwise_conv(a * jax.nn.sigmoid(g), w_dw, b_dw)
    return jax.nn.silu(layernorm(y, g_n, b_n))


def odd_mixer(h_ctx, h_lat, cos, sin, w_in, lam_q1, lam_k1, lam_q2, lam_k2, g_subln, conv_w, conv_b,
              conv_norm_g, conv_norm_b, w_out, lam_init, need_ctx):
    def project(h, rope):
        b, n = h.shape[:2]
        q, k, v, pc = jnp.split(h @ w_in, [ODD_QK, 2 * ODD_QK, 2 * ODD_QK + ODD_V], axis=-1)
        q = q.reshape(b, n, DIFF_HEADS, 2, DIFF_QK)
        k = k.reshape(b, n, DIFF_HEADS, 2, DIFF_QK)
        v = v.reshape(b, n, DIFF_HEADS, DIFF_V)
        qk = [q[..., 0, :], q[..., 1, :], k[..., 0, :], k[..., 1, :]]
        if rope:
            qk = [apply_rope(t, cos, sin) for t in qk]
        return qk[0], qk[1], qk[2], qk[3], v, pc

    f32 = lambda t: t.astype(jnp.float32)
    lam = (jnp.exp(jnp.sum(f32(lam_q1) * f32(lam_k1))) - jnp.exp(jnp.sum(f32(lam_q2) * f32(lam_k2)))
           + lam_init)
    q1c, q2c, k1c, k2c, vc, pcc = project(h_ctx, False)
    q1l, q2l, k1l, k2l, vl, pcl = project(h_lat, True)
    k1a = jnp.concatenate([k1c, k1l], axis=1)
    k2a = jnp.concatenate([k2c, k2l], axis=1)
    va = jnp.concatenate([vc, vl], axis=1)

    def heads_out(o):
        o = rmsnorm(o, g_subln) * (1.0 - lam_init)
        return o.reshape(o.shape[:2] + (ODD_V,))

    o_l = heads_out(sweep_query_blocks(lambda a, b: diff_attend(a, b, k1a, k2a, va, lam), q1l, q2l))
    cv_l = conformer_conv(pcl, conv_w, conv_b, conv_norm_g, conv_norm_b)
    y_l = jnp.concatenate([o_l, cv_l], axis=-1) @ w_out
    y_c = None
    if need_ctx:
        o_c = heads_out(diff_attend(q1c, q2c, k1c, k2c, vc, lam))
        cv_c = conformer_conv(pcc, conv_w, conv_b, conv_norm_g, conv_norm_b)
        y_c = jnp.concatenate([o_c, cv_c], axis=-1) @ w_out
    return y_c, y_l


def setup_inputs(seed: int = 0) -> dict:
    key = jax.random.key(seed)
    ks = iter(jax.random.split(key, 48))
    D = D_MODEL

    def normal(shape):
        return jax.random.normal(next(ks), shape, jnp.float32)

    def dense(shape, fan_in, scale=1.0):
        return normal(shape) * (scale * fan_in ** -0.5)

    def gain(shape):
        return 1.0 + 0.05 * normal(shape)

    def small(shape, s=0.02):
        return s * normal(shape)

    inp = {}
    inp['x'] = normal((BATCH, SEQ, D))
    inp['c'] = normal((BATCH, D))
    inp['ctx'] = normal((BATCH, CTX_LEN, D))
    inp['c_ctx'] = normal((D,))
    inp['w_mod'] = dense((DEPTH, D, N_MOD * D), D, 0.5)
    inp['b_mod'] = small((DEPTH, N_MOD * D))
    inp['g_mix_pre'] = gain((DEPTH, D))
    inp['g_mix_post'] = gain((DEPTH, D))
    inp['g_ffn_pre'] = gain((DEPTH, D))
    inp['g_ffn_post'] = gain((DEPTH, D))
    inp['w_ff1'] = dense((DEPTH, D, D_FF), D)
    inp['w_ff3'] = dense((DEPTH, D, D_FF), D)
    inp['w_ff2'] = dense((DEPTH, D_FF, D), D_FF)
    inp['e_w_in'] = dense((N_EVEN, D, EVEN_IN), D)
    inp['e_g_q'] = gain((N_EVEN, MLA_Q_RANK))
    inp['e_w_uq'] = dense((N_EVEN, MLA_Q_RANK, MLA_HEADS * (MLA_NOPE + MLA_ROPE)), MLA_Q_RANK)
    inp['e_g_kv'] = gain((N_EVEN, MLA_KV_RANK))
    inp['e_w_ukv'] = dense((N_EVEN, MLA_KV_RANK, MLA_HEADS * (MLA_NOPE + MLA_V)), MLA_KV_RANK)
    inp['e_s5_lam_re'] = -0.5 + 0.01 * normal((N_EVEN, 2, S5_GROUPS, S5_STATE))
    inp['e_s5_lam_im'] = (math.pi * jnp.arange(S5_STATE, dtype=jnp.float32)
                          + 0.01 * normal((N_EVEN, 2, S5_GROUPS, S5_STATE)))
    inp['e_s5_log_dt'] = jax.random.uniform(next(ks), (N_EVEN, 2, S5_GROUPS), jnp.float32,
                                            minval=math.log(1e-3), maxval=math.log(1e-1))
    inp['e_s5_b_re'] = dense((N_EVEN, 2, S5_GROUPS, S5_STATE, S5_GROUP), 2 * S5_GROUP)
    inp['e_s5_b_im'] = dense((N_EVEN, 2, S5_GROUPS, S5_STATE, S5_GROUP), 2 * S5_GROUP)
    inp['e_s5_c_re'] = dense((N_EVEN, 2, S5_GROUPS, S5_GROUP, S5_STATE), 2 * S5_STATE)
    inp['e_s5_c_im'] = dense((N_EVEN, 2, S5_GROUPS, S5_GROUP, S5_STATE), 2 * S5_STATE)
    inp['e_s5_d'] = normal((N_EVEN, S5_WIDTH))
    inp['e_w_glu'] = dense((N_EVEN, S5_WIDTH, 2 * S5_WIDTH), S5_WIDTH)
    inp['e_w_out'] = dense((N_EVEN, EVEN_OUT, D), EVEN_OUT)
    inp['o_w_in'] = dense((N_ODD, D, ODD_IN), D)
    inp['o_lam_q1'] = small((N_ODD, DIFF_QK), 0.1)
    inp['o_lam_k1'] = small((N_ODD, DIFF_QK), 0.1)
    inp['o_lam_q2'] = small((N_ODD, DIFF_QK), 0.1)
    inp['o_lam_k2'] = small((N_ODD, DIFF_QK), 0.1)
    inp['o_g_subln'] = gain((N_ODD, DIFF_V))
    inp['o_conv_w'] = dense((N_ODD, CONV_W, CONV_CH), CONV_W)
    inp['o_conv_b'] = small((N_ODD, CONV_CH))
    inp['o_conv_norm_g'] = gain((N_ODD, CONV_CH))
    inp['o_conv_norm_b'] = small((N_ODD, CONV_CH))
    inp['o_w_out'] = dense((N_ODD, ODD_OUT, D), ODD_OUT)
    return inp


def reference(x, c, ctx, c_ctx, w_mod, b_mod, g_mix_pre, g_mix_post, g_ffn_pre, g_ffn_post,
              w_ff1, w_ff3, w_ff2, e_w_in, e_g_q, e_w_uq, e_g_kv, e_w_ukv, e_s5_lam_re, e_s5_lam_im,
              e_s5_log_dt, e_s5_b_re, e_s5_b_im, e_s5_c_re, e_s5_c_im, e_s5_d, e_w_glu, e_w_out,
              o_w_in, o_lam_q1, o_lam_k1, o_lam_q2, o_lam_k2, o_g_subln, o_conv_w, o_conv_b,
              o_conv_norm_g, o_conv_norm_b, o_w_out):
    n = x.shape[1]
    cos, sin = axial_rope_tables(n, MLA_ROPE)
    silu_c = jax.nn.silu(c)
    silu_cc = jax.nn.silu(c_ctx)
    for layer in range(DEPTH):
        need_ctx = layer < DEPTH - 1
        mod_l = (silu_c @ w_mod[layer] + b_mod[layer])[:, None, :]
        mod_c = (silu_cc @ w_mod[layer] + b_mod[layer])[None, None, :]
        sh_m, sc_m, gt_m, sh_f, sc_f, gt_f = jnp.split(mod_l, N_MOD, axis=-1)
        csh_m, csc_m, cgt_m, csh_f, csc_f, cgt_f = jnp.split(mod_c, N_MOD, axis=-1)
        h_lat = rmsnorm(x, g_mix_pre[layer]) * (1.0 + sc_m) + sh_m
        h_ctx = rmsnorm(ctx, g_mix_pre[layer]) * (1.0 + csc_m) + csh_m
        i = layer // 2
        if layer % 2 == 0:
            y_ctx, y_lat = even_mixer(h_ctx, h_lat, cos, sin, e_w_in[i], e_g_q[i], e_w_uq[i], e_g_kv[i],
                                      e_w_ukv[i], e_s5_lam_re[i], e_s5_lam_im[i], e_s5_log_dt[i],
                                      e_s5_b_re[i], e_s5_b_im[i], e_s5_c_re[i], e_s5_c_im[i], e_s5_d[i],
                                      e_w_glu[i], e_w_out[i], need_ctx)
        else:
            lam_init = 0.8 - 0.6 * math.exp(-0.3 * layer)
            y_ctx, y_lat = odd_mixer(h_ctx, h_lat, cos, sin, o_w_in[i], o_lam_q1[i], o_lam_k1[i],
                                     o_lam_q2[i], o_lam_k2[i], o_g_subln[i], o_conv_w[i], o_conv_b[i],
                                     o_conv_norm_g[i], o_conv_norm_b[i], o_w_out[i], lam_init, need_ctx)
        x = x + gt_m * rmsnorm(y_lat, g_mix_post[layer])
        h = rmsnorm(x, g_ffn_pre[layer]) * (1.0 + sc_f) + sh_f
        x = x + gt_f * rmsnorm(swiglu(h, w_ff1[layer], w_ff3[layer], w_ff2[layer]), g_ffn_post[layer])
        if need_ctx:
            ctx = ctx + cgt_m * rmsnorm(y_ctx, g_mix_post[layer])
            hc = rmsnorm(ctx, g_ffn_pre[layer]) * (1.0 + csc_f) + csh_f
            ctx = ctx + cgt_f * rmsnorm(swiglu(hc, w_ff1[layer], w_ff3[layer], w_ff2[layer]),
                                        g_ffn_post[layer])
    return x
```

```python
import functools
import math

import jax
import jax.numpy as jnp
from jax import lax
from jax.experimental import pallas as pl
from jax.experimental.pallas import tpu as pltpu

F32 = jnp.float32
BF16 = jnp.bfloat16

EPS = 1e-6
ROPE_BASE = 10000.0
GRID_W = 64

MLA_HEADS = 16
MLA_NOPE = 128
MLA_ROPE = 64
MLA_V = 128
MLA_HEAD_PAD = 256
DIFF_HEADS = 16
DIFF_QK = 64
DIFF_V = 128
S5_GROUP = 16
S5_GROUPS = 64
S5_STATE = 64
S5_BLOCKS = 8
N_MOD = 6

LANES = 128
SUBLANES = 8
VMEM_LIMIT = 56 * 1024 * 1024

SH_M, SC_M, GT_M, SH_F, SC_F, GT_F = range(N_MOD)


def _params(*sem):
    return pltpu.CompilerParams(dimension_semantics=sem, vmem_limit_bytes=VMEM_LIMIT)


def _row_tile(rows, cap):
    k = -(-rows // cap)
    while rows % k or (rows // k) % 16:
        k += 1
    return rows // k


def _mod_kernel(c_ref, w_ref, b_ref, o_ref):
    c = c_ref[...]
    a = (c * jax.nn.sigmoid(c)).astype(BF16)
    o_ref[...] = jnp.dot(a, w_ref[...].astype(BF16), preferred_element_type=F32) + b_ref[...]


def _modulation(cvec, w_mod, b_mod):
    depth, d, n = w_mod.shape
    tn = 512
    return pl.pallas_call(
        _mod_kernel,
        out_shape=jax.ShapeDtypeStruct((depth, SUBLANES, n), F32),
        grid=(depth, n // tn),
        in_specs=[pl.BlockSpec((SUBLANES, d), lambda l, j: (0, 0)),
                  pl.BlockSpec((None, d, tn), lambda l, j: (l, 0, j)),
                  pl.BlockSpec((None, 1, tn), lambda l, j: (l, 0, j))],
        out_specs=pl.BlockSpec((None, SUBLANES, tn), lambda l, j: (l, 0, j)),
        compiler_params=_params("parallel", "parallel"),
    )(cvec, w_mod, b_mod.reshape(depth, 1, n))


def _norm_mod_kernel(x_ref, g_ref, m_ref, o_ref, *, shift_row, scale_row):
    x = x_ref[...]
    r = lax.rsqrt(jnp.mean(x * x, axis=-1, keepdims=True) + EPS)
    h = x * r * g_ref[...]
    o_ref[...] = (h * (1.0 + m_ref[scale_row:scale_row + 1, :])
                  + m_ref[shift_row:shift_row + 1, :]).astype(o_ref.dtype)


def _norm_mod(x, gain, modtab, shift_row, scale_row, seg_offset, tr):
    b, rows, d = x.shape
    return pl.pallas_call(
        functools.partial(_norm_mod_kernel, shift_row=shift_row, scale_row=scale_row),
        out_shape=jax.ShapeDtypeStruct((b, rows, d), BF16),
        grid=(b, rows // tr),
        in_specs=[pl.BlockSpec((None, tr, d), lambda bi, i: (bi, i, 0)),
                  pl.BlockSpec((1, d), lambda bi, i: (0, 0)),
                  pl.BlockSpec((None, None, N_MOD, d),
                               lambda bi, i: (bi, jnp.minimum(i + seg_offset, 1), 0, 0))],
        out_specs=pl.BlockSpec((None, tr, d), lambda bi, i: (bi, i, 0)),
        compiler_params=_params("parallel", "parallel"),
    )(x, gain.reshape(1, d), modtab)


def _residual_kernel(x_ref, y_ref, g_ref, m_ref, o_ref, *, gate_row):
    y = y_ref[...].astype(F32)
    r = lax.rsqrt(jnp.mean(y * y, axis=-1, keepdims=True) + EPS)
    o_ref[...] = x_ref[...] + m_ref[gate_row:gate_row + 1, :] * (y * r * g_ref[...])


def _residual(x, y, gain, modtab, gate_row, x_tile_offset, seg_offset, tr):
    b, rows, d = y.shape
    return pl.pallas_call(
        functools.partial(_residual_kernel, gate_row=gate_row),
        out_shape=jax.ShapeDtypeStruct((b, rows, d), F32),
        grid=(b, rows // tr),
        in_specs=[pl.BlockSpec((None, tr, d), lambda bi, i: (bi, i + x_tile_offset, 0)),
                  pl.BlockSpec((None, tr, d), lambda bi, i: (bi, i, 0)),
                  pl.BlockSpec((1, d), lambda bi, i: (0, 0)),
                  pl.BlockSpec((None, None, N_MOD, d),
                               lambda bi, i: (bi, jnp.minimum(i + seg_offset, 1), 0, 0))],
        out_specs=pl.BlockSpec((None, tr, d), lambda bi, i: (bi, i, 0)),
        compiler_params=_params("parallel", "parallel"),
    )(x, y, gain.reshape(1, d), modtab)


def _matmul_kernel(*refs, n_pairs, normed):
    o_ref = refs[-1]
    acc = None
    for p in range(n_pairs):
        a = refs[2 * p][...]
        if normed:
            af = a.astype(F32)
            r = lax.rsqrt(jnp.mean(af * af, axis=-1, keepdims=True) + EPS)
            a = (af * r * refs[2 * n_pairs][...]).astype(BF16)
        part = jnp.dot(a, refs[2 * p + 1][...], preferred_element_type=F32)
        acc = part if acc is None else acc + part
    o_ref[...] = acc.astype(o_ref.dtype)


def _matmul(pairs, tm, tn, gain=None, out_dtype=BF16):
    rows = pairs[0][0].shape[0]
    n = pairs[0][2].shape[1]
    tn = min(tn, n)
    in_specs, args = [], []
    for a, cb, w in pairs:
        k = w.shape[0]
        in_specs.append(pl.BlockSpec((tm, k), functools.partial(lambda i, j, cb: (i, cb), cb=cb)))
        in_specs.append(pl.BlockSpec((k, tn), lambda i, j: (0, j)))
        args += [a, w]
    if gain is not None:
        k = pairs[0][2].shape[0]
        in_specs.append(pl.BlockSpec((1, k), lambda i, j: (0, 0)))
        args.append(gain.reshape(1, k).astype(F32))
    return pl.pallas_call(
        functools.partial(_matmul_kernel, n_pairs=len(pairs), normed=gain is not None),
        out_shape=jax.ShapeDtypeStruct((rows, n), out_dtype),
        grid=(rows // tm, n // tn),
        in_specs=in_specs,
        out_specs=pl.BlockSpec((tm, tn), lambda i, j: (i, j)),
        compiler_params=_params("parallel", "arbitrary"),
    )(*args)


def _swiglu_kernel(a_ref, w_ref, o_ref, *, tn):
    r = jnp.dot(a_ref[...], w_ref[...], preferred_element_type=F32)
    g = r[:, :tn]
    o_ref[...] = (g * jax.nn.sigmoid(g) * r[:, tn:]).astype(o_ref.dtype)


def _swiglu(a, w13, tm, tn):
    rows, k = a.shape
    f = w13.shape[1] // 2
    return pl.pallas_call(
        functools.partial(_swiglu_kernel, tn=tn),
        out_shape=jax.ShapeDtypeStruct((rows, f), BF16),
        grid=(rows // tm, f // tn),
        in_specs=[pl.BlockSpec((tm, k), lambda i, j: (i, 0)),
                  pl.BlockSpec((k, 2 * tn), lambda i, j: (0, j))],
        out_specs=pl.BlockSpec((tm, tn), lambda i, j: (i, j)),
        compiler_params=_params("parallel", "arbitrary"),
    )(a, w13)


def _rope_tables(n_tok, dim):
    rows = n_tok // GRID_W
    row = jnp.broadcast_to(jnp.arange(rows, dtype=F32)[:, None], (rows, GRID_W)).reshape(-1)
    col = jnp.broadcast_to(jnp.arange(GRID_W, dtype=F32)[None, :], (rows, GRID_W)).reshape(-1)
    axis_dim = dim // 2
    inv = ROPE_BASE ** (-jnp.arange(0, axis_dim, 2, dtype=F32) / axis_dim)
    ang_r = row[:, None] * inv[None, :]
    ang_c = col[:, None] * inv[None, :]
    ang = jnp.concatenate([ang_r, ang_r, ang_c, ang_c], axis=-1)
    return jnp.cos(ang), jnp.sin(ang)


def _rot_matrix(dim):
    q = dim // 4
    p = jnp.zeros((dim, dim), F32)
    eye = jnp.eye(q, dtype=F32)
    p = p.at[q:2 * q, 0:q].set(-eye)
    p = p.at[0:q, q:2 * q].set(eye)
    p = p.at[3 * q:, 2 * q:3 * q].set(-eye)
    p = p.at[2 * q:3 * q, 3 * q:].set(eye)
    return p


def _embed(block, shape, r0, c0, fill):
    out = jnp.full(shape, fill, F32)
    return out.at[r0:r0 + block.shape[0], c0:c0 + block.shape[1]].set(block)


def _rope(x_bf16, cos, sin, rot):
    turned = jnp.dot(x_bf16, rot, preferred_element_type=F32)
    return x_bf16.astype(F32) * cos + turned * sin


def _softmax_parts(s):
    m = jnp.max(s, axis=-1, keepdims=True)
    e = jnp.exp(s - m)
    return e, jnp.sum(e, axis=-1, keepdims=True)


_NT = (((1,), (1,)), ((), ()))


def _mla_kernel(q_ref, kv_ref, kr_ref, cq_ref, sq_ref, ck_ref, sk_ref, pq_ref, pk_ref,
                o_ref, kcat_ref, *, n_ctx, scale):
    qi = pl.program_id(2)

    @pl.when(qi == 0)
    def _():
        kr = _rope(kr_ref[...], ck_ref[...], sk_ref[...], pk_ref[...])
        kcat_ref[:, :MLA_NOPE] = kv_ref[:, :MLA_NOPE]
        kcat_ref[:, MLA_NOPE:] = kr.astype(BF16)

    q = (_rope(q_ref[...], cq_ref[...], sq_ref[...], pq_ref[...]) * scale).astype(BF16)

    def attend(k, v):
        s = lax.dot_general(q, k, _NT, preferred_element_type=F32)
        e, l = _softmax_parts(s)
        o = jnp.dot(e.astype(BF16), v, preferred_element_type=F32)
        return (o / l).astype(o_ref.dtype)

    @pl.when(qi == 0)
    def _():
        o_ref[...] = attend(kcat_ref[:n_ctx, :], kv_ref[:n_ctx, MLA_NOPE:])

    @pl.when(qi > 0)
    def _():
        o_ref[...] = attend(kcat_ref[...], kv_ref[:, MLA_NOPE:])


def _mla_attention(q, kv, hin, kr_block, tabs, n_ctx):
    b, t, _ = q.shape
    tq = n_ctx
    cq, sq, ck, sk, pq, pk = tabs
    scale = (MLA_NOPE + MLA_ROPE) ** -0.5
    hp = MLA_HEAD_PAD
    return pl.pallas_call(
        functools.partial(_mla_kernel, n_ctx=n_ctx, scale=scale),
        out_shape=jax.ShapeDtypeStruct((b, t, MLA_HEADS * MLA_V), BF16),
        grid=(b, MLA_HEADS, t // tq),
        in_specs=[pl.BlockSpec((None, tq, hp), lambda bi, h, i: (bi, i, h)),
                  pl.BlockSpec((None, t, hp), lambda bi, h, i: (bi, 0, h)),
                  pl.BlockSpec((None, t, LANES), lambda bi, h, i: (bi, 0, kr_block)),
                  pl.BlockSpec((tq, hp), lambda bi, h, i: (i, 0)),
                  pl.BlockSpec((tq, hp), lambda bi, h, i: (i, 0)),
                  pl.BlockSpec((t, LANES), lambda bi, h, i: (0, 0)),
                  pl.BlockSpec((t, LANES), lambda bi, h, i: (0, 0)),
                  pl.BlockSpec((hp, hp), lambda bi, h, i: (0, 0)),
                  pl.BlockSpec((LANES, LANES), lambda bi, h, i: (0, 0))],
        out_specs=pl.BlockSpec((None, tq, MLA_V), lambda bi, h, i: (bi, i, h)),
        scratch_shapes=[pltpu.VMEM((t, hp), BF16)],
        compiler_params=_params("parallel", "parallel", "arbitrary"),
    )(q, kv, hin, cq, sq, ck, sk, pq, pk)


def _diff_kernel(q_ref, k_ref, v_ref, cq_ref, sq_ref, ck_ref, sk_ref, p_ref, lam_ref, g_ref,
                 o_ref, kro_ref, *, scale, lam_init):
    qi = pl.program_id(2)

    @pl.when(qi == 0)
    def _():
        kro_ref[...] = _rope(k_ref[...], ck_ref[...], sk_ref[...], p_ref[...]).astype(BF16)

    lv = lam_ref[...]
    lam = (jnp.exp(jnp.sum(lv[0:1] * lv[1:2], axis=-1, keepdims=True))
           - jnp.exp(jnp.sum(lv[2:3] * lv[3:4], axis=-1, keepdims=True)) + lam_init)

    q = _rope(q_ref[...], cq_ref[...], sq_ref[...], p_ref[...]) * scale
    lane = lax.broadcasted_iota(jnp.int32, q.shape, 1)
    q1 = jnp.where(lane < DIFF_QK, q, 0.0).astype(BF16)
    q2 = jnp.where(lane >= DIFF_QK, q, 0.0).astype(BF16)
    k = kro_ref[...]
    e1, l1 = _softmax_parts(lax.dot_general(q1, k, _NT, preferred_element_type=F32))
    e2, l2 = _softmax_parts(lax.dot_general(q2, k, _NT, preferred_element_type=F32))
    p = e1 * (1.0 / l1) - e2 * (lam / l2)
    o = jnp.dot(p.astype(BF16), v_ref[...], preferred_element_type=F32)
    r = lax.rsqrt(jnp.mean(o * o, axis=-1, keepdims=True) + EPS)
    o_ref[...] = (o * r * g_ref[...] * (1.0 - lam_init)).astype(o_ref.dtype)


def _diff_attention(hin, tabs, lam_vecs, g_subln, lam_init, n_ctx):
    b, t, _ = hin.shape
    tq = n_ctx
    n = t - n_ctx
    cos, sin, p = tabs
    h_all = DIFF_HEADS
    return pl.pallas_call(
        functools.partial(_diff_kernel, scale=DIFF_QK ** -0.5, lam_init=lam_init),
        out_shape=jax.ShapeDtypeStruct((b, n, h_all * DIFF_V), BF16),
        grid=(b, h_all, n // tq),
        in_specs=[pl.BlockSpec((None, tq, LANES), lambda bi, h, i: (bi, i + 1, h)),
                  pl.BlockSpec((None, t, LANES), lambda bi, h, i: (bi, 0, h_all + h)),
                  pl.BlockSpec((None, t, LANES), lambda bi, h, i: (bi, 0, 2 * h_all + h)),
                  pl.BlockSpec((tq, LANES), lambda bi, h, i: (i + 1, 0)),
                  pl.BlockSpec((tq, LANES), lambda bi, h, i: (i + 1, 0)),
                  pl.BlockSpec((t, LANES), lambda bi, h, i: (0, 0)),
                  pl.BlockSpec((t, LANES), lambda bi, h, i: (0, 0)),
                  pl.BlockSpec((LANES, LANES), lambda bi, h, i: (0, 0)),
                  pl.BlockSpec((4, DIFF_QK), lambda bi, h, i: (0, 0)),
                  pl.BlockSpec((1, DIFF_V), lambda bi, h, i: (0, 0))],
        out_specs=pl.BlockSpec((None, tq, DIFF_V), lambda bi, h, i: (bi, i, h)),
        scratch_shapes=[pltpu.VMEM((t, LANES), BF16)],
        compiler_params=_params("parallel", "parallel", "arbitrary"),
    )(hin, hin, hin, cos, sin, cos, sin, p, lam_vecs, g_subln.reshape(1, DIFF_V).astype(F32))


def _s5_kernel(u_ref, wb_ref, wc_ref, are_ref, aim_ref, y_ref, s_ref, h_ref, *, tc):
    d = pl.program_id(1)
    half = S5_GROUPS // S5_BLOCKS * S5_STATE
    gw = S5_GROUPS // S5_BLOCKS * S5_GROUP
    nl = half // LANES

    @pl.when(pl.program_id(2) == 0)
    def _():
        h_ref[...] = jnp.zeros_like(h_ref)

    for j in range(S5_BLOCKS):
        bu = jnp.dot(u_ref[:, j * gw:(j + 1) * gw], wb_ref[j], preferred_element_type=F32)
        for c in range(2 * nl):
            s_ref.at[c][pl.ds(j, tc, stride=S5_BLOCKS), :] = bu[:, c * LANES:(c + 1) * LANES]

    a_re = [are_ref[:, c * LANES:(c + 1) * LANES] for c in range(nl)]
    a_im = [aim_ref[:, c * LANES:(c + 1) * LANES] for c in range(nl)]

    def step(t, h):
        tt = jnp.where(d == 0, t, tc - 1 - t)
        row = pl.multiple_of(tt * S5_BLOCKS, S5_BLOCKS)
        out = [None] * (2 * nl)
        for c in range(nl):
            h_re, h_im = h[c], h[nl + c]
            out[c] = a_re[c] * h_re - a_im[c] * h_im + s_ref[c, pl.ds(row, S5_BLOCKS), :]
            out[nl + c] = a_re[c] * h_im + a_im[c] * h_re + s_ref[nl + c, pl.ds(row, S5_BLOCKS), :]
        for c in range(2 * nl):
            s_ref[c, pl.ds(row, S5_BLOCKS), :] = out[c]
        return tuple(out)

    h = lax.fori_loop(0, tc, step, tuple(h_ref[c] for c in range(2 * nl)), unroll=4)
    for c in range(2 * nl):
        h_ref[c] = h[c]

    for j in range(S5_BLOCKS):
        hj = jnp.concatenate([s_ref.at[c][pl.ds(j, tc, stride=S5_BLOCKS), :] for c in range(2 * nl)],
                             axis=1).astype(BF16)
        y_ref[:, j * gw:(j + 1) * gw] = jnp.dot(hj, wc_ref[j], preferred_element_type=F32)


def _s5_scan(hin, u_block, wb, wc, a_re, a_im, tc):
    b, t, _ = hin.shape
    width = S5_GROUPS * S5_GROUP
    half = S5_GROUPS // S5_BLOCKS * S5_STATE
    nchunk = t // tc

    def chunk(d, k):
        return jnp.where(d == 0, k, jnp.where(k == 0, 0, nchunk - k))

    return pl.pallas_call(
        functools.partial(_s5_kernel, tc=tc),
        out_shape=jax.ShapeDtypeStruct((b, 2, t, width), F32),
        grid=(b, 2, nchunk),
        in_specs=[pl.BlockSpec((None, tc, width), lambda bi, d, k: (bi, chunk(d, k), u_block)),
                  pl.BlockSpec((None,) + wb.shape[1:], lambda bi, d, k: (d, 0, 0, 0)),
                  pl.BlockSpec((None,) + wc.shape[1:], lambda bi, d, k: (d, 0, 0, 0)),
                  pl.BlockSpec((None, S5_BLOCKS, half), lambda bi, d, k: (d, 0, 0)),
                  pl.BlockSpec((None, S5_BLOCKS, half), lambda bi, d, k: (d, 0, 0))],
        out_specs=pl.BlockSpec((None, None, tc, width), lambda bi, d, k: (bi, d, chunk(d, k), 0)),
        scratch_shapes=[pltpu.VMEM((2 * half // LANES, tc * S5_BLOCKS, LANES), F32),
                        pltpu.VMEM((2 * half // LANES, S5_BLOCKS, LANES), F32)],
        compiler_params=_params("parallel", "parallel", "arbitrary"),
    )(hin, wb, wc, a_re, a_im)


def _s5_weights(lam_re, lam_im, log_dt, b_re, b_im, c_re, c_im):
    lr = lam_re.astype(F32)
    li = lam_im.astype(F32)
    dt = jnp.exp(log_dt.astype(F32))[..., None]
    mag = jnp.exp(lr * dt)
    ab_re = mag * jnp.cos(li * dt)
    ab_im = mag * jnp.sin(li * dt)
    den = lr * lr + li * li
    num_re = ab_re - 1.0
    coef_re = ((num_re * lr + ab_im * li) / den)[..., None]
    coef_im = ((ab_im * lr - num_re * li) / den)[..., None]
    br = b_re.astype(F32)
    bi = b_im.astype(F32)
    bb_re = coef_re * br - coef_im * bi
    bb_im = coef_re * bi + coef_im * br
    nb = S5_BLOCKS
    gb = S5_GROUPS // nb
    eye = jnp.eye(gb, dtype=F32)

    def drive(bb):
        w = jnp.einsum('djgpi,gh->djgihp', bb.reshape(2, nb, gb, S5_STATE, S5_GROUP), eye)
        return w.reshape(2, nb, gb * S5_GROUP, gb * S5_STATE)

    def read(c):
        w = jnp.einsum('djgip,gh->djgphi', c.astype(F32).reshape(2, nb, gb, S5_GROUP, S5_STATE), eye)
        return w.reshape(2, nb, gb * S5_STATE, gb * S5_GROUP)

    wb = jnp.concatenate([drive(bb_re), drive(bb_im)], axis=-1).astype(BF16)
    wc = jnp.concatenate([read(c_re), -read(c_im)], axis=-2).astype(BF16)
    return wb, wc, ab_re.reshape(2, nb, gb * S5_STATE), ab_im.reshape(2, nb, gb * S5_STATE)


def _s5_glu_kernel(yf_ref, yb_ref, u_ref, d_ref, w_ref, o_ref):
    y = yf_ref[...] + yb_ref[...] + d_ref[...] * u_ref[...].astype(F32)
    r = jnp.dot(jax.nn.gelu(y).astype(BF16), w_ref[...], preferred_element_type=F32)
    n = r.shape[1] // 2
    o_ref[...] = (r[:, :n] * jax.nn.sigmoid(r[:, n:])).astype(o_ref.dtype)


def _s5_glu(y4, hin, u_block, d_skip, w_glu, tm):
    b, _, t, width = y4.shape
    return pl.pallas_call(
        _s5_glu_kernel,
        out_shape=jax.ShapeDtypeStruct((b, t, width), BF16),
        grid=(b, t // tm),
        in_specs=[pl.BlockSpec((None, None, tm, width), lambda bi, i: (bi, 0, i, 0)),
                  pl.BlockSpec((None, None, tm, width), lambda bi, i: (bi, 1, i, 0)),
                  pl.BlockSpec((None, tm, width), lambda bi, i: (bi, i, u_block)),
                  pl.BlockSpec((1, width), lambda bi, i: (0, 0)),
                  pl.BlockSpec(w_glu.shape, lambda bi, i: (0, 0))],
        out_specs=pl.BlockSpec((None, tm, width), lambda bi, i: (bi, i, 0)),
        compiler_params=_params("parallel", "parallel"),
    )(y4, y4, hin, d_skip.reshape(1, width).astype(F32), w_glu)


CONV_HALO = 16


def _conv_kernel(ap_ref, gp_ref, ac_ref, gc_ref, an_ref, gn_ref, w_ref, b_ref, lg_ref, lb_ref,
                 o_ref, v_ref, y_ref, *, tr, taps):
    i = pl.program_id(1)
    last = pl.num_programs(1) - 1

    def glu(a_ref, g_ref):
        return a_ref[...].astype(F32) * jax.nn.sigmoid(g_ref[...].astype(F32))

    v_ref[0:CONV_HALO, :] = jnp.where(i > 0, glu(ap_ref, gp_ref), 0.0)
    v_ref[CONV_HALO:CONV_HALO + tr, :] = glu(ac_ref, gc_ref)
    v_ref[CONV_HALO + tr:, :] = jnp.where(i < last, glu(an_ref, gn_ref), 0.0)

    first = CONV_HALO - taps // 2
    for c0 in range(0, v_ref.shape[1], LANES):
        acc = jnp.zeros((tr, LANES), F32)
        for k in range(taps):
            acc = acc + w_ref[k:k + 1, c0:c0 + LANES] * v_ref[first + k:first + k + tr, c0:c0 + LANES]
        y_ref[:, c0:c0 + LANES] = acc

    y = y_ref[...] + b_ref[...]
    yc = y - jnp.mean(y, axis=-1, keepdims=True)
    yn = yc * lax.rsqrt(jnp.mean(yc * yc, axis=-1, keepdims=True) + EPS) * lg_ref[...] + lb_ref[...]
    o_ref[...] = (yn * jax.nn.sigmoid(yn)).astype(o_ref.dtype)


def _conformer_conv(hin, a_block, w_dw, b_dw, g_n, b_n, n_ctx, tr):
    b, t, _ = hin.shape
    n = t - n_ctx
    taps, ch = w_dw.shape
    hb = tr // CONV_HALO
    off = n_ctx // tr
    last_halo = t // CONV_HALO - 1

    def prev_map(cb):
        return lambda bi, i: (bi, (i + off) * hb - 1, cb)

    def next_map(cb):
        return lambda bi, i: (bi, jnp.minimum((i + off + 1) * hb, last_halo), cb)

    def cur_map(cb):
        return lambda bi, i: (bi, i + off, cb)

    vec = lambda v: v.reshape(1, ch).astype(F32)
    halo = lambda m: pl.BlockSpec((None, CONV_HALO, ch), m)
    full = lambda shape: pl.BlockSpec(shape, lambda bi, i: (0, 0))
    return pl.pallas_call(
        functools.partial(_conv_kernel, tr=tr, taps=taps),
        out_shape=jax.ShapeDtypeStruct((b, n, ch), BF16),
        grid=(b, n // tr),
        in_specs=[halo(prev_map(a_block)), halo(prev_map(a_block + 1)),
                  pl.BlockSpec((None, tr, ch), cur_map(a_block)),
                  pl.BlockSpec((None, tr, ch), cur_map(a_block + 1)),
                  halo(next_map(a_block)), halo(next_map(a_block + 1)),
                  full((taps, ch)), full((1, ch)), full((1, ch)), full((1, ch))],
        out_specs=pl.BlockSpec((None, tr, ch), lambda bi, i: (bi, i, 0)),
        scratch_shapes=[pltpu.VMEM((tr + 2 * CONV_HALO, ch), F32), pltpu.VMEM((tr, ch), F32)],
        compiler_params=_params("parallel", "parallel"),
    )(hin, hin, hin, hin, hin, hin, w_dw.astype(F32), vec(b_dw), vec(g_n), vec(b_n))


def _ffn(z, modtab, seg_offset, g_pre, g_post, w13, w2, tr, tn_ff):
    b, rows, d = z.shape
    h = _norm_mod(z, g_pre, modtab, SH_F, SC_F, seg_offset, tr).reshape(b * rows, d)
    act = _swiglu(h, w13, _row_tile(b * rows, 1100), tn_ff)
    y = _matmul([(act, 0, w2)], _row_tile(b * rows, 550), 256)
    return _residual(z, y.reshape(b, rows, d), g_post, modtab, GT_F, 0, seg_offset, tr)


def kernel(x, c, ctx, c_ctx, w_mod, b_mod, g_mix_pre, g_mix_post, g_ffn_pre, g_ffn_post, w_ff1, w_ff3, w_ff2, e_w_in, e_g_q, e_w_uq, e_g_kv, e_w_ukv, e_s5_lam_re, e_s5_lam_im, e_s5_log_dt, e_s5_b_re, e_s5_b_im, e_s5_c_re, e_s5_c_im, e_s5_d, e_w_glu, e_w_out, o_w_in, o_lam_q1, o_lam_k1, o_lam_q2, o_lam_k2, o_g_subln, o_conv_w, o_conv_b, o_conv_norm_g, o_conv_norm_b, o_w_out):
    b, n, d = x.shape
    n_ctx = ctx.shape[1]
    t = n_ctx + n
    depth = w_mod.shape[0]
    assert depth == 2 and e_w_in.shape[0] == 1 and o_w_in.shape[0] == 1
    assert b + 1 <= SUBLANES and n % n_ctx == 0
    tr = n_ctx
    d_ff = w_ff1.shape[2]
    tn_ff = 256
    q_rank = e_g_q.shape[1]
    kv_rank = e_g_kv.shape[1]
    s5_w = S5_GROUPS * S5_GROUP

    cvec = jnp.zeros((SUBLANES, d), F32).at[:b].set(c).at[b].set(c_ctx)
    mod = _modulation(cvec, w_mod, b_mod)
    modtabs = []
    for layer in range(depth):
        lat = mod[layer, :b].reshape(b, 1, N_MOD, d)
        cx = jnp.broadcast_to(mod[layer, b].reshape(1, 1, N_MOD, d), (b, 1, N_MOD, d))
        modtabs.append(jnp.concatenate([cx, lat], axis=1))

    cos, sin = _rope_tables(n, MLA_ROPE)
    rot = _rot_matrix(MLA_ROPE)
    hp = MLA_HEAD_PAD
    mla_tabs = (_embed(cos, (t, hp), n_ctx, MLA_NOPE, 1.0), _embed(sin, (t, hp), n_ctx, MLA_NOPE, 0.0),
                _embed(cos, (t, LANES), n_ctx, 0, 1.0), _embed(sin, (t, LANES), n_ctx, 0, 0.0),
                _embed(rot, (hp, hp), MLA_NOPE, MLA_NOPE, 0.0).astype(BF16),
                _embed(rot, (LANES, LANES), 0, 0, 0.0).astype(BF16))
    cos2 = jnp.concatenate([cos, cos], axis=1)
    sin2 = jnp.concatenate([sin, sin], axis=1)
    rot2 = _embed(rot, (LANES, LANES), 0, 0, 0.0).at[DIFF_QK:, DIFF_QK:].set(rot)
    diff_tabs = (_embed(cos2, (t, LANES), n_ctx, 0, 1.0), _embed(sin2, (t, LANES), n_ctx, 0, 0.0),
                 rot2.astype(BF16))

    z = jnp.concatenate([ctx, x], axis=1)

    lw = e_w_in[0]
    cuts = [q_rank, q_rank + kv_rank, q_rank + kv_rank + MLA_ROPE]
    w_in = jnp.concatenate([lw[:, :cuts[0]], lw[:, cuts[2]:], lw[:, cuts[0]:cuts[1]], lw[:, cuts[1]:cuts[2]],
                            jnp.zeros((d, LANES - MLA_ROPE), F32)], axis=1).astype(BF16)
    u_block = q_rank // s5_w
    kv_block = (q_rank + s5_w) // kv_rank
    kr_block = (q_rank + s5_w + kv_rank) // LANES
    w_uq = e_w_uq[0].reshape(q_rank, MLA_HEADS, MLA_NOPE + MLA_ROPE)
    w_uq = jnp.pad(w_uq, ((0, 0), (0, 0), (0, hp - MLA_NOPE - MLA_ROPE))).reshape(q_rank, MLA_HEADS * hp)
    w_uq = w_uq.astype(BF16)
    w_ukv = e_w_ukv[0].astype(BF16)
    w_out = e_w_out[0].astype(BF16)
    n_attn = MLA_HEADS * MLA_V

    h = _norm_mod(z, g_mix_pre[0], modtabs[0], SH_M, SC_M, 0, tr).reshape(b * t, d)
    tm = _row_tile(b * t, 1100)
    hin = _matmul([(h, 0, w_in)], tm, w_in.shape[1] // 3)
    q = _matmul([(hin, 0, w_uq)], tm, 1024, gain=e_g_q[0])
    kv = _matmul([(hin, kv_block, w_ukv)], tm, 1024, gain=e_g_kv[0])
    hin3 = hin.reshape(b, t, -1)
    attn = _mla_attention(q.reshape(b, t, -1), kv.reshape(b, t, -1), hin3, kr_block, mla_tabs, n_ctx)
    wb, wc, a_re, a_im = _s5_weights(e_s5_lam_re[0], e_s5_lam_im[0], e_s5_log_dt[0], e_s5_b_re[0],
                                     e_s5_b_im[0], e_s5_c_re[0], e_s5_c_im[0])
    y4 = _s5_scan(hin3, u_block, wb, wc, a_re, a_im, tr)
    s5 = _s5_glu(y4, hin3, u_block, e_s5_d[0], e_w_glu[0].astype(BF16), tr)
    y = _matmul([(attn.reshape(b * t, -1), 0, w_out[:n_attn]), (s5.reshape(b * t, -1), 0, w_out[n_attn:])],
                tm, 512)
    z = _residual(z, y.reshape(b, t, d), g_mix_post[0], modtabs[0], GT_M, 0, 0, tr)

    def ff_weights(layer):
        nt = d_ff // tn_ff
        w13 = jnp.concatenate([w_ff1[layer].reshape(d, nt, tn_ff), w_ff3[layer].reshape(d, nt, tn_ff)],
                              axis=2).reshape(d, 2 * d_ff).astype(BF16)
        return w13, w_ff2[layer].astype(BF16)

    z = _ffn(z, modtabs[0], 0, g_ffn_pre[0], g_ffn_post[0], *ff_weights(0), tr, tn_ff)

    lam_init = 0.8 - 0.6 * math.exp(-0.3 * 1)
    w_in = o_w_in[0].astype(BF16)
    w_out = o_w_out[0].astype(BF16)
    n_attn = DIFF_HEADS * DIFF_V
    conv_ch = o_conv_w.shape[2]

    h = _norm_mod(z, g_mix_pre[1], modtabs[1], SH_M, SC_M, 0, tr).reshape(b * t, d)
    hin3 = _matmul([(h, 0, w_in)], tm, 512).reshape(b, t, -1)
    lam_vecs = jnp.stack([o_lam_q1[0], o_lam_k1[0], o_lam_q2[0], o_lam_k2[0]]).astype(F32)
    attn = _diff_attention(hin3, diff_tabs, lam_vecs, o_g_subln[0], lam_init, n_ctx)
    conv = _conformer_conv(hin3, 3 * n_attn // conv_ch, o_conv_w[0], o_conv_b[0], o_conv_norm_g[0],
                           o_conv_norm_b[0], n_ctx, tr)
    tml = _row_tile(b * n, 1100)
    y = _matmul([(attn.reshape(b * n, -1), 0, w_out[:n_attn]), (conv.reshape(b * n, -1), 0, w_out[n_attn:])],
                tml, 512)
    xl = _residual(z, y.reshape(b, n, d), g_mix_post[1], modtabs[1], GT_M, n_ctx // tr, 1, tr)
    return _ffn(xl, modtabs[1], 1, g_ffn_pre[1], g_ffn_post[1], *ff_weights(1), tr, tn_ff)
```

```python
import functools
import math

import numpy as np

import jax
import jax.numpy as jnp
from jax import lax
from jax.experimental import pallas as pl
from jax.experimental.pallas import tpu as pltpu

F32 = jnp.float32
BF16 = jnp.bfloat16

EPS = 1e-6
ROPE_BASE = 10000.0
GRID_W = 64
LOG2E = math.log2(math.e)

MLA_HEADS = 16
MLA_NOPE = 128
MLA_ROPE = 64
MLA_V = 128
MLA_HEAD_PAD = 256
MLA_HEADS_PER_STEP = 2
DIFF_HEADS = 16
DIFF_QK = 64
DIFF_V = 128
S5_GROUP = 16
S5_GROUPS = 64
S5_STATE = 64
S5_BLOCKS = 8
N_MOD = 6

LANES = 128
SUBLANES = 8
VMEM_LIMIT = 56 * 1024 * 1024

SH_M, SC_M, GT_M, SH_F, SC_F, GT_F = range(N_MOD)


def _params(*sem):
    return pltpu.CompilerParams(dimension_semantics=sem, vmem_limit_bytes=VMEM_LIMIT)


def _row_tile(rows, cap):
    k = -(-rows // cap)
    while rows % k or (rows // k) % 16:
        k += 1
    return rows // k


def _mod_kernel(c_ref, w_ref, b_ref, o_ref):
    c = c_ref[...]
    a = (c * jax.nn.sigmoid(c)).astype(BF16)
    o_ref[...] = jnp.dot(a, w_ref[...].astype(BF16), preferred_element_type=F32) + b_ref[...]


def _modulation(cvec, w_mod, b_mod):
    depth, d, n = w_mod.shape
    tn = 512
    return pl.pallas_call(
        _mod_kernel,
        out_shape=jax.ShapeDtypeStruct((depth, SUBLANES, n), F32),
        grid=(depth, n // tn),
        in_specs=[pl.BlockSpec((SUBLANES, d), lambda l, j: (0, 0)),
                  pl.BlockSpec((None, d, tn), lambda l, j: (l, 0, j)),
                  pl.BlockSpec((None, 1, tn), lambda l, j: (l, 0, j))],
        out_specs=pl.BlockSpec((None, SUBLANES, tn), lambda l, j: (l, 0, j)),
        compiler_params=_params("parallel", "parallel"),
    )(cvec, w_mod, b_mod.reshape(depth, 1, n))


def _stream_specs(src, tr, d, tile_offset):
    if isinstance(src, tuple):
        return ([pl.BlockSpec((None, tr, d), lambda bi, i: (bi, 0, 0)),
                 pl.BlockSpec((None, tr, d), lambda bi, i: (bi, jnp.maximum(i - 1, 0), 0))], list(src))
    return [pl.BlockSpec((None, tr, d), lambda bi, i: (bi, i + tile_offset, 0))], [src]


def _stream_rows(x_refs):
    if len(x_refs) == 1:
        return x_refs[0][...]
    return jnp.where(pl.program_id(1) == 0, x_refs[0][...], x_refs[1][...])


def _mod_spec(d, seg_offset):
    return pl.BlockSpec((None, None, N_MOD, d), lambda bi, i: (bi, jnp.minimum(i + seg_offset, 1), 0, 0))


def _adaln(x, g_ref, m_ref, shift_row, scale_row):
    r = lax.rsqrt(jnp.mean(x * x, axis=-1, keepdims=True) + EPS)
    h = x * r * g_ref[...]
    return (h * (1.0 + m_ref[scale_row:scale_row + 1, :]) + m_ref[shift_row:shift_row + 1, :]).astype(BF16)


def _norm_mod_kernel(*refs, n_src, shift_row, scale_row):
    g_ref, m_ref, o_ref = refs[n_src:]
    o_ref[...] = _adaln(_stream_rows(refs[:n_src]), g_ref, m_ref, shift_row, scale_row)


def _norm_mod(src, gain, modtab, shift_row, scale_row, tr):
    specs, args = _stream_specs(src, tr, gain.shape[0], 0)
    b = args[0].shape[0]
    d = gain.shape[0]
    rows = sum(a.shape[1] for a in args)
    return pl.pallas_call(
        functools.partial(_norm_mod_kernel, n_src=len(args), shift_row=shift_row, scale_row=scale_row),
        out_shape=jax.ShapeDtypeStruct((b, rows, d), BF16),
        grid=(b, rows // tr),
        in_specs=specs + [pl.BlockSpec((1, d), lambda bi, i: (0, 0)), _mod_spec(d, 0)],
        out_specs=pl.BlockSpec((None, tr, d), lambda bi, i: (bi, i, 0)),
        compiler_params=_params("parallel", "parallel"),
    )(*args, gain.reshape(1, d), modtab)


def _residual_kernel(*refs, n_src, gate_row, nxt_rows):
    y_ref, g_ref, m_ref = refs[n_src:n_src + 3]
    y = y_ref[...].astype(F32)
    r = lax.rsqrt(jnp.mean(y * y, axis=-1, keepdims=True) + EPS)
    z = _stream_rows(refs[:n_src]) + m_ref[gate_row:gate_row + 1, :] * (y * r * g_ref[...])
    if nxt_rows is None:
        refs[-1][...] = z
    else:
        g2_ref, m2_ref, o_ref, h_ref = refs[n_src + 3:]
        o_ref[...] = z
        h_ref[...] = _adaln(z, g2_ref, m2_ref, *nxt_rows)


def _residual(src, y, gain, modtab, gate_row, x_tile_offset, seg_offset, tr, nxt=None):
    b, rows, d = y.shape
    specs, args = _stream_specs(src, tr, d, x_tile_offset)
    tile = lambda: pl.BlockSpec((None, tr, d), lambda bi, i: (bi, i, 0))
    vec = lambda: pl.BlockSpec((1, d), lambda bi, i: (0, 0))
    in_specs = specs + [tile(), vec(), _mod_spec(d, seg_offset)]
    args = args + [y, gain.reshape(1, d), modtab]
    out_shape = jax.ShapeDtypeStruct((b, rows, d), F32)
    out_specs = tile()
    nxt_rows = None
    if nxt is not None:
        in_specs += [vec(), _mod_spec(d, seg_offset)]
        args += [nxt[0].reshape(1, d), nxt[1]]
        nxt_rows = (nxt[2], nxt[3])
        out_shape = (out_shape, jax.ShapeDtypeStruct((b, rows, d), BF16))
        out_specs = (out_specs, tile())
    return pl.pallas_call(
        functools.partial(_residual_kernel, n_src=len(specs), gate_row=gate_row, nxt_rows=nxt_rows),
        out_shape=out_shape,
        grid=(b, rows // tr),
        in_specs=in_specs,
        out_specs=out_specs,
        compiler_params=_params("parallel", "parallel"),
    )(*args)


def _matmul_kernel(*refs, n_pairs, normed):
    o_ref = refs[-1]
    acc = None
    for p in range(n_pairs):
        a = refs[2 * p][...]
        if normed:
            af = a.astype(F32)
            r = lax.rsqrt(jnp.mean(af * af, axis=-1, keepdims=True) + EPS)
            a = (af * r * refs[2 * n_pairs][...]).astype(BF16)
        part = jnp.dot(a, refs[2 * p + 1][...], preferred_element_type=F32)
        acc = part if acc is None else acc + part
    o_ref[...] = acc.astype(o_ref.dtype)


def _weight_spec(w, layer, tn):
    if w.ndim == 3:
        return pl.BlockSpec((None, w.shape[1], tn), lambda i, j: (layer, 0, j))
    return pl.BlockSpec((w.shape[0], tn), lambda i, j: (0, j))


def _matmul(pairs, tm, tn, gain=None, layer=0, out_dtype=BF16):
    rows = pairs[0][0].shape[0]
    n = pairs[0][2].shape[-1]
    tn = min(tn, n)
    in_specs, args = [], []
    for a, cb, w in pairs:
        k = w.shape[-2]
        in_specs.append(pl.BlockSpec((tm, k), functools.partial(lambda i, j, cb: (i, cb), cb=cb)))
        in_specs.append(_weight_spec(w, layer, tn))
        args += [a, w]
    if gain is not None:
        k = pairs[0][2].shape[-2]
        in_specs.append(pl.BlockSpec((1, k), lambda i, j: (0, 0)))
        args.append(gain.reshape(1, k).astype(F32))
    return pl.pallas_call(
        functools.partial(_matmul_kernel, n_pairs=len(pairs), normed=gain is not None),
        out_shape=jax.ShapeDtypeStruct((rows, n), out_dtype),
        grid=(rows // tm, n // tn),
        in_specs=in_specs,
        out_specs=pl.BlockSpec((tm, tn), lambda i, j: (i, j)),
        compiler_params=_params("parallel", "arbitrary"),
    )(*args)


def _swiglu_kernel(a_ref, w1_ref, w3_ref, o_ref):
    a = a_ref[...]
    g = jnp.dot(a, w1_ref[...].astype(BF16), preferred_element_type=F32)
    u = jnp.dot(a, w3_ref[...].astype(BF16), preferred_element_type=F32)
    o_ref[...] = (g * jax.nn.sigmoid(g) * u).astype(o_ref.dtype)


def _swiglu(a, w1, w3, layer, tm, tn):
    rows, k = a.shape
    f = w1.shape[2]
    return pl.pallas_call(
        _swiglu_kernel,
        out_shape=jax.ShapeDtypeStruct((rows, f), BF16),
        grid=(rows // tm, f // tn),
        in_specs=[pl.BlockSpec((tm, k), lambda i, j: (i, 0)),
                  _weight_spec(w1, layer, tn), _weight_spec(w3, layer, tn)],
        out_specs=pl.BlockSpec((tm, tn), lambda i, j: (i, j)),
        compiler_params=_params("parallel", "arbitrary"),
    )(a, w1, w3)


def _rope_tables(n_tok, dim):
    rows = n_tok // GRID_W
    row = np.repeat(np.arange(rows, dtype=np.float32), GRID_W)
    col = np.tile(np.arange(GRID_W, dtype=np.float32), rows)
    axis_dim = dim // 2
    expo = -np.arange(0, axis_dim, 2, dtype=np.float32) / np.float32(axis_dim)
    inv = np.power(np.float32(ROPE_BASE), expo).astype(np.float32)
    ang_r = row[:, None] * inv[None, :]
    ang_c = col[:, None] * inv[None, :]
    ang = np.concatenate([ang_r, ang_r, ang_c, ang_c], axis=-1).astype(np.float32)
    return np.cos(ang), np.sin(ang)


def _rot_matrix(dim):
    q = dim // 4
    p = np.zeros((dim, dim), np.float32)
    eye = np.eye(q, dtype=np.float32)
    p[q:2 * q, 0:q] = -eye
    p[0:q, q:2 * q] = eye
    p[3 * q:, 2 * q:3 * q] = -eye
    p[2 * q:3 * q, 3 * q:] = eye
    return p


def _embed(block, shape, r0, c0, fill):
    out = np.full(shape, fill, np.float32)
    out[r0:r0 + block.shape[0], c0:c0 + block.shape[1]] = block
    return out


def _rope(x_bf16, cos, sin, rot):
    turned = jnp.dot(x_bf16, rot, preferred_element_type=F32)
    return x_bf16.astype(F32) * cos + turned * sin


KEY_CHUNK = 1024


def _softmax_values(scores, vones):
    keys = scores[0].shape[1]
    maxes = [jnp.max(s, axis=-1, keepdims=True) for s in scores]
    accs = [None] * len(scores)
    for c0 in range(0, keys, KEY_CHUNK):
        c1 = min(c0 + KEY_CHUNK, keys)
        for i, (s, m, vone) in enumerate(zip(scores, maxes, vones)):
            e = jnp.exp2(s[:, c0:c1] - m).astype(BF16)
            part = jnp.dot(e, vone[c0:c1, :], preferred_element_type=F32)
            accs[i] = part if accs[i] is None else accs[i] + part
    dv = accs[0].shape[1] // 2
    return [a[:, :dv] / a[:, dv:] for a in accs]


def _const_spec(shape):
    return pl.BlockSpec(shape, lambda bi, h, i: (0,) * len(shape), pipeline_mode=pl.Buffered(1))


_NT = (((1,), (1,)), ((), ()))


def _mla_kernel(q_ref, kv_ref, kr_ref, cq_ref, sq_ref, ck_ref, sk_ref, pq_ref, pk_ref,
                o_ref, kcat_ref, vone_ref, *, n_ctx, scale, heads):
    qi = pl.program_id(2)
    hp = MLA_HEAD_PAD

    @pl.when(qi == 0)
    def _():
        kr = _rope(kr_ref[...], ck_ref[...], sk_ref[...], pk_ref[...]).astype(BF16)
        ones = jnp.ones((kr.shape[0], hp - MLA_V), BF16)
        for h in range(heads):
            kcat_ref[h, :, :MLA_NOPE] = kv_ref[:, h * hp:h * hp + MLA_NOPE]
            kcat_ref[h, :, MLA_NOPE:] = kr
            vone_ref[h, :, :MLA_V] = kv_ref[:, h * hp + MLA_NOPE:(h + 1) * hp]
            vone_ref[h, :, MLA_V:] = ones

    def attend(rows):
        scores = []
        for h in range(heads):
            q = _rope(q_ref[:, h * hp:(h + 1) * hp], cq_ref[...], sq_ref[...], pq_ref[...]) * scale
            scores.append(lax.dot_general(q.astype(BF16), kcat_ref[h, :rows, :], _NT,
                                          preferred_element_type=F32))
        outs = _softmax_values(scores, [vone_ref.at[h, :rows, :] for h in range(heads)])
        for h in range(heads):
            o_ref[:, h * MLA_V:(h + 1) * MLA_V] = outs[h].astype(o_ref.dtype)

    @pl.when(qi == 0)
    def _():
        attend(n_ctx)

    @pl.when(qi > 0)
    def _():
        attend(kcat_ref.shape[1])


def _mla_attention(q, kv, hin, kr_block, tabs, n_ctx):
    b, t, _ = q.shape
    tq = n_ctx
    cq, sq, ck, sk, pq, pk = tabs
    scale = (MLA_NOPE + MLA_ROPE) ** -0.5 * LOG2E
    hp = MLA_HEAD_PAD
    hs = MLA_HEADS_PER_STEP
    assert hp == 2 * MLA_V and hp == MLA_NOPE + LANES
    return pl.pallas_call(
        functools.partial(_mla_kernel, n_ctx=n_ctx, scale=scale, heads=hs),
        out_shape=jax.ShapeDtypeStruct((b, t, MLA_HEADS * MLA_V), BF16),
        grid=(b, MLA_HEADS // hs, t // tq),
        in_specs=[pl.BlockSpec((None, tq, hs * hp), lambda bi, h, i: (bi, i, h)),
                  pl.BlockSpec((None, t, hs * hp), lambda bi, h, i: (bi, 0, h)),
                  pl.BlockSpec((None, t, LANES), lambda bi, h, i: (bi, 0, kr_block)),
                  pl.BlockSpec((tq, hp), lambda bi, h, i: (i, 0)),
                  pl.BlockSpec((tq, hp), lambda bi, h, i: (i, 0)),
                  _const_spec((t, LANES)), _const_spec((t, LANES)),
                  _const_spec((hp, hp)), _const_spec((LANES, LANES))],
        out_specs=pl.BlockSpec((None, tq, hs * MLA_V), lambda bi, h, i: (bi, i, h)),
        scratch_shapes=[pltpu.VMEM((hs, t, hp), BF16), pltpu.VMEM((hs, t, hp), BF16)],
        compiler_params=_params("parallel", "parallel", "arbitrary"),
    )(q, kv, hin, cq, sq, ck, sk, pq, pk)


def _diff_kernel(q_ref, k_ref, v_ref, cq_ref, sq_ref, ck_ref, sk_ref, p_ref, lam_ref, g_ref,
                 o_ref, kro_ref, vone_ref, *, scale, lam_init):
    qi = pl.program_id(2)

    @pl.when(qi == 0)
    def _():
        kro_ref[...] = _rope(k_ref[...], ck_ref[...], sk_ref[...], p_ref[...]).astype(BF16)
        vone_ref[:, :DIFF_V] = v_ref[...]
        vone_ref[:, DIFF_V:] = jnp.ones((v_ref.shape[0], DIFF_V), BF16)

    lv = lam_ref[...]
    lam = (jnp.exp(jnp.sum(lv[0:1] * lv[1:2], axis=-1, keepdims=True))
           - jnp.exp(jnp.sum(lv[2:3] * lv[3:4], axis=-1, keepdims=True)) + lam_init)

    q = _rope(q_ref[...], cq_ref[...], sq_ref[...], p_ref[...]) * scale
    lane = lax.broadcasted_iota(jnp.int32, q.shape, 1)

    def scores(qm):
        return lax.dot_general(qm.astype(BF16), kro_ref[...], _NT, preferred_element_type=F32)

    o1, o2 = _softmax_values([scores(jnp.where(lane < DIFF_QK, q, 0.0)),
                              scores(jnp.where(lane >= DIFF_QK, q, 0.0))], [vone_ref, vone_ref])
    o = o1 - lam * o2
    r = lax.rsqrt(jnp.mean(o * o, axis=-1, keepdims=True) + EPS)
    o_ref[...] = (o * r * g_ref[...] * (1.0 - lam_init)).astype(o_ref.dtype)


def _diff_attention(hin, tabs, lam_vecs, g_subln, lam_init, n_ctx):
    b, t, _ = hin.shape
    tq = n_ctx
    n = t - n_ctx
    cos, sin, p = tabs
    h_all = DIFF_HEADS
    return pl.pallas_call(
        functools.partial(_diff_kernel, scale=DIFF_QK ** -0.5 * LOG2E, lam_init=lam_init),
        out_shape=jax.ShapeDtypeStruct((b, n, h_all * DIFF_V), BF16),
        grid=(b, h_all, n // tq),
        in_specs=[pl.BlockSpec((None, tq, LANES), lambda bi, h, i: (bi, i + 1, h)),
                  pl.BlockSpec((None, t, LANES), lambda bi, h, i: (bi, 0, h_all + h)),
                  pl.BlockSpec((None, t, LANES), lambda bi, h, i: (bi, 0, 2 * h_all + h)),
                  pl.BlockSpec((tq, LANES), lambda bi, h, i: (i + 1, 0)),
                  pl.BlockSpec((tq, LANES), lambda bi, h, i: (i + 1, 0)),
                  _const_spec((t, LANES)), _const_spec((t, LANES)),
                  _const_spec((LANES, LANES)), _const_spec((4, DIFF_QK)), _const_spec((1, DIFF_V))],
        out_specs=pl.BlockSpec((None, tq, DIFF_V), lambda bi, h, i: (bi, i, h)),
        scratch_shapes=[pltpu.VMEM((t, LANES), BF16), pltpu.VMEM((t, 2 * DIFF_V), BF16)],
        compiler_params=_params("parallel", "parallel", "arbitrary"),
    )(hin, hin, hin, cos, sin, cos, sin, p, lam_vecs, g_subln.reshape(1, DIFF_V).astype(F32))


def _s5_kernel(u_ref, wb_ref, wc_ref, are_ref, aim_ref, y_ref, s_ref, h_ref, *, tc):
    d = pl.program_id(1)
    half = S5_GROUPS // S5_BLOCKS * S5_STATE
    gw = S5_GROUPS // S5_BLOCKS * S5_GROUP
    nl = half // LANES

    @pl.when(pl.program_id(2) == 0)
    def _():
        h_ref[...] = jnp.zeros_like(h_ref)

    for j in range(S5_BLOCKS):
        bu = jnp.dot(u_ref[:, j * gw:(j + 1) * gw], wb_ref[j], preferred_element_type=F32)
        for c in range(2 * nl):
            s_ref.at[c][pl.ds(j, tc, stride=S5_BLOCKS), :] = bu[:, c * LANES:(c + 1) * LANES]

    a_re = [are_ref[:, c * LANES:(c + 1) * LANES] for c in range(nl)]
    a_im = [aim_ref[:, c * LANES:(c + 1) * LANES] for c in range(nl)]

    def step(t, h):
        tt = jnp.where(d == 0, t, tc - 1 - t)
        row = pl.multiple_of(tt * S5_BLOCKS, S5_BLOCKS)
        out = [None] * (2 * nl)
        for c in range(nl):
            h_re, h_im = h[c], h[nl + c]
            out[c] = a_re[c] * h_re - a_im[c] * h_im + s_ref[c, pl.ds(row, S5_BLOCKS), :]
            out[nl + c] = a_re[c] * h_im + a_im[c] * h_re + s_ref[nl + c, pl.ds(row, S5_BLOCKS), :]
        for c in range(2 * nl):
            s_ref[c, pl.ds(row, S5_BLOCKS), :] = out[c]
        return tuple(out)

    h = lax.fori_loop(0, tc, step, tuple(h_ref[c] for c in range(2 * nl)), unroll=4)
    for c in range(2 * nl):
        h_ref[c] = h[c]

    for j in range(S5_BLOCKS):
        hj = jnp.concatenate([s_ref.at[c][pl.ds(j, tc, stride=S5_BLOCKS), :] for c in range(2 * nl)],
                             axis=1).astype(BF16)
        y_ref[:, j * gw:(j + 1) * gw] = jnp.dot(hj, wc_ref[j], preferred_element_type=F32)


def _s5_scan(hin, u_block, wb, wc, a_re, a_im, tc):
    b, t, _ = hin.shape
    width = S5_GROUPS * S5_GROUP
    half = S5_GROUPS // S5_BLOCKS * S5_STATE
    nchunk = t // tc

    def chunk(d, k):
        return jnp.where(d == 0, k, jnp.where(k == 0, 0, nchunk - k))

    return pl.pallas_call(
        functools.partial(_s5_kernel, tc=tc),
        out_shape=jax.ShapeDtypeStruct((b, 2, t, width), F32),
        grid=(b, 2, nchunk),
        in_specs=[pl.BlockSpec((None, tc, width), lambda bi, d, k: (bi, chunk(d, k), u_block)),
                  pl.BlockSpec((None,) + wb.shape[1:], lambda bi, d, k: (d, 0, 0, 0)),
                  pl.BlockSpec((None,) + wc.shape[1:], lambda bi, d, k: (d, 0, 0, 0)),
                  pl.BlockSpec((None, S5_BLOCKS, half), lambda bi, d, k: (d, 0, 0)),
                  pl.BlockSpec((None, S5_BLOCKS, half), lambda bi, d, k: (d, 0, 0))],
        out_specs=pl.BlockSpec((None, None, tc, width), lambda bi, d, k: (bi, d, chunk(d, k), 0)),
        scratch_shapes=[pltpu.VMEM((2 * half // LANES, tc * S5_BLOCKS, LANES), F32),
                        pltpu.VMEM((2 * half // LANES, S5_BLOCKS, LANES), F32)],
        compiler_params=_params("parallel", "parallel", "arbitrary"),
    )(hin, wb, wc, a_re, a_im)


def _s5_weights(lam_re, lam_im, log_dt, b_re, b_im, c_re, c_im):
    lr = lam_re.astype(F32)
    li = lam_im.astype(F32)
    dt = jnp.exp(log_dt.astype(F32))[..., None]
    mag = jnp.exp(lr * dt)
    ab_re = mag * jnp.cos(li * dt)
    ab_im = mag * jnp.sin(li * dt)
    den = lr * lr + li * li
    num_re = ab_re - 1.0
    coef_re = ((num_re * lr + ab_im * li) / den)[..., None]
    coef_im = ((ab_im * lr - num_re * li) / den)[..., None]
    br = b_re.astype(F32)
    bi = b_im.astype(F32)
    bb_re = coef_re * br - coef_im * bi
    bb_im = coef_re * bi + coef_im * br
    nb = S5_BLOCKS
    gb = S5_GROUPS // nb
    eye = jnp.eye(gb, dtype=F32)

    def drive(bb):
        w = jnp.einsum('djgpi,gh->djgihp', bb.reshape(2, nb, gb, S5_STATE, S5_GROUP), eye)
        return w.reshape(2, nb, gb * S5_GROUP, gb * S5_STATE)

    def read(c):
        w = jnp.einsum('djgip,gh->djgphi', c.astype(F32).reshape(2, nb, gb, S5_GROUP, S5_STATE), eye)
        return w.reshape(2, nb, gb * S5_STATE, gb * S5_GROUP)

    wb = jnp.concatenate([drive(bb_re), drive(bb_im)], axis=-1).astype(BF16)
    wc = jnp.concatenate([read(c_re), -read(c_im)], axis=-2).astype(BF16)
    return wb, wc, ab_re.reshape(2, nb, gb * S5_STATE), ab_im.reshape(2, nb, gb * S5_STATE)


def _s5_glu_kernel(yf_ref, yb_ref, u_ref, d_ref, w_ref, o_ref):
    y = yf_ref[...] + yb_ref[...] + d_ref[...] * u_ref[...].astype(F32)
    r = jnp.dot(jax.nn.gelu(y).astype(BF16), w_ref[...], preferred_element_type=F32)
    n = r.shape[1] // 2
    o_ref[...] = (r[:, :n] * jax.nn.sigmoid(r[:, n:])).astype(o_ref.dtype)


def _s5_glu(y4, hin, u_block, d_skip, w_glu, tm):
    b, _, t, width = y4.shape
    return pl.pallas_call(
        _s5_glu_kernel,
        out_shape=jax.ShapeDtypeStruct((b, t, width), BF16),
        grid=(b, t // tm),
        in_specs=[pl.BlockSpec((None, None, tm, width), lambda bi, i: (bi, 0, i, 0)),
                  pl.BlockSpec((None, None, tm, width), lambda bi, i: (bi, 1, i, 0)),
                  pl.BlockSpec((None, tm, width), lambda bi, i: (bi, i, u_block)),
                  pl.BlockSpec((1, width), lambda bi, i: (0, 0)),
                  pl.BlockSpec(w_glu.shape, lambda bi, i: (0, 0))],
        out_specs=pl.BlockSpec((None, tm, width), lambda bi, i: (bi, i, 0)),
        compiler_params=_params("parallel", "parallel"),
    )(y4, y4, hin, d_skip.reshape(1, width).astype(F32), w_glu)


CONV_HALO = 16


def _conv_kernel(ap_ref, gp_ref, ac_ref, gc_ref, an_ref, gn_ref, w_ref, b_ref, lg_ref, lb_ref,
                 o_ref, v_ref, y_ref, *, tr, taps):
    i = pl.program_id(1)
    last = pl.num_programs(1) - 1

    def glu(a_ref, g_ref):
        return a_ref[...].astype(F32) * jax.nn.sigmoid(g_ref[...].astype(F32))

    v_ref[0:CONV_HALO, :] = jnp.where(i > 0, glu(ap_ref, gp_ref), 0.0)
    v_ref[CONV_HALO:CONV_HALO + tr, :] = glu(ac_ref, gc_ref)
    v_ref[CONV_HALO + tr:, :] = jnp.where(i < last, glu(an_ref, gn_ref), 0.0)

    first = CONV_HALO - taps // 2
    for c0 in range(0, v_ref.shape[1], LANES):
        acc = jnp.zeros((tr, LANES), F32)
        for k in range(taps):
            acc = acc + w_ref[k:k + 1, c0:c0 + LANES] * v_ref[first + k:first + k + tr, c0:c0 + LANES]
        y_ref[:, c0:c0 + LANES] = acc

    y = y_ref[...] + b_ref[...]
    yc = y - jnp.mean(y, axis=-1, keepdims=True)
    yn = yc * lax.rsqrt(jnp.mean(yc * yc, axis=-1, keepdims=True) + EPS) * lg_ref[...] + lb_ref[...]
    o_ref[...] = (yn * jax.nn.sigmoid(yn)).astype(o_ref.dtype)


def _conformer_conv(hin, a_block, w_dw, b_dw, g_n, b_n, n_ctx, tr):
    b, t, _ = hin.shape
    n = t - n_ctx
    taps, ch = w_dw.shape
    hb = tr // CONV_HALO
    off = n_ctx // tr
    last_halo = t // CONV_HALO - 1

    def prev_map(cb):
        return lambda bi, i: (bi, (i + off) * hb - 1, cb)

    def next_map(cb):
        return lambda bi, i: (bi, jnp.minimum((i + off + 1) * hb, last_halo), cb)

    def cur_map(cb):
        return lambda bi, i: (bi, i + off, cb)

    vec = lambda v: v.reshape(1, ch).astype(F32)
    halo = lambda m: pl.BlockSpec((None, CONV_HALO, ch), m)
    full = lambda shape: pl.BlockSpec(shape, lambda bi, i: (0, 0))
    return pl.pallas_call(
        functools.partial(_conv_kernel, tr=tr, taps=taps),
        out_shape=jax.ShapeDtypeStruct((b, n, ch), BF16),
        grid=(b, n // tr),
        in_specs=[halo(prev_map(a_block)), halo(prev_map(a_block + 1)),
                  pl.BlockSpec((None, tr, ch), cur_map(a_block)),
                  pl.BlockSpec((None, tr, ch), cur_map(a_block + 1)),
                  halo(next_map(a_block)), halo(next_map(a_block + 1)),
                  full((taps, ch)), full((1, ch)), full((1, ch)), full((1, ch))],
        out_specs=pl.BlockSpec((None, tr, ch), lambda bi, i: (bi, i, 0)),
        scratch_shapes=[pltpu.VMEM((tr + 2 * CONV_HALO, ch), F32), pltpu.VMEM((tr, ch), F32)],
        compiler_params=_params("parallel", "parallel"),
    )(hin, hin, hin, hin, hin, hin, w_dw.astype(F32), vec(b_dw), vec(g_n), vec(b_n))


def _ffn_matmuls(h, w1, w3, w2, layer):
    b, rows, d = h.shape
    act = _swiglu(h.reshape(b * rows, d), w1, w3, layer, _row_tile(b * rows, 1100), 256)
    y = _matmul([(act, 0, w2)], _row_tile(b * rows, 550), 256, layer=layer)
    return y.reshape(b, rows, d)


def kernel(x, c, ctx, c_ctx, w_mod, b_mod, g_mix_pre, g_mix_post, g_ffn_pre, g_ffn_post, w_ff1, w_ff3, w_ff2, e_w_in, e_g_q, e_w_uq, e_g_kv, e_w_ukv, e_s5_lam_re, e_s5_lam_im, e_s5_log_dt, e_s5_b_re, e_s5_b_im, e_s5_c_re, e_s5_c_im, e_s5_d, e_w_glu, e_w_out, o_w_in, o_lam_q1, o_lam_k1, o_lam_q2, o_lam_k2, o_g_subln, o_conv_w, o_conv_b, o_conv_norm_g, o_conv_norm_b, o_w_out):
    b, n, d = x.shape
    n_ctx = ctx.shape[1]
    t = n_ctx + n
    depth = w_mod.shape[0]
    assert depth == 2 and e_w_in.shape[0] == 1 and o_w_in.shape[0] == 1
    assert b + 1 <= SUBLANES and n % n_ctx == 0
    tr = n_ctx
    q_rank = e_g_q.shape[1]
    kv_rank = e_g_kv.shape[1]
    s5_w = S5_GROUPS * S5_GROUP

    cvec = jnp.zeros((SUBLANES, d), F32).at[:b].set(c).at[b].set(c_ctx)
    mod = _modulation(cvec, w_mod, b_mod)
    modtabs = []
    for layer in range(depth):
        lat = mod[layer, :b].reshape(b, 1, N_MOD, d)
        cx = jnp.broadcast_to(mod[layer, b].reshape(1, 1, N_MOD, d), (b, 1, N_MOD, d))
        modtabs.append(jnp.concatenate([cx, lat], axis=1))

    cos, sin = _rope_tables(n, MLA_ROPE)
    rot = _rot_matrix(MLA_ROPE)
    hp = MLA_HEAD_PAD
    mla_tabs = (_embed(cos, (t, hp), n_ctx, MLA_NOPE, 1.0), _embed(sin, (t, hp), n_ctx, MLA_NOPE, 0.0),
                _embed(cos, (t, LANES), n_ctx, 0, 1.0), _embed(sin, (t, LANES), n_ctx, 0, 0.0),
                jnp.asarray(_embed(rot, (hp, hp), MLA_NOPE, MLA_NOPE, 0.0), BF16),
                jnp.asarray(_embed(rot, (LANES, LANES), 0, 0, 0.0), BF16))
    cos2 = np.concatenate([cos, cos], axis=1)
    sin2 = np.concatenate([sin, sin], axis=1)
    rot2 = _embed(rot, (LANES, LANES), 0, 0, 0.0)
    rot2[DIFF_QK:, DIFF_QK:] = rot
    diff_tabs = (_embed(cos2, (t, LANES), n_ctx, 0, 1.0), _embed(sin2, (t, LANES), n_ctx, 0, 0.0),
                 jnp.asarray(rot2, BF16))

    w_ff2b = w_ff2.astype(BF16)

    lw = e_w_in[0]
    cuts = [q_rank, q_rank + kv_rank, q_rank + kv_rank + MLA_ROPE]
    w_in = jnp.concatenate([lw[:, :cuts[0]], lw[:, cuts[2]:], lw[:, cuts[0]:cuts[1]], lw[:, cuts[1]:cuts[2]],
                            jnp.zeros((d, LANES - MLA_ROPE), F32)], axis=1).astype(BF16)
    u_block = q_rank // s5_w
    kv_block = (q_rank + s5_w) // kv_rank
    kr_block = (q_rank + s5_w + kv_rank) // LANES
    w_uq = e_w_uq[0].reshape(q_rank, MLA_HEADS, MLA_NOPE + MLA_ROPE)
    w_uq = jnp.pad(w_uq, ((0, 0), (0, 0), (0, hp - MLA_NOPE - MLA_ROPE))).reshape(q_rank, MLA_HEADS * hp)
    w_uq = w_uq.astype(BF16)
    w_ukv = e_w_ukv[0].astype(BF16)
    w_out = e_w_out[0].astype(BF16)
    n_attn = MLA_HEADS * MLA_V

    h = _norm_mod((ctx, x), g_mix_pre[0], modtabs[0], SH_M, SC_M, tr).reshape(b * t, d)
    tm = _row_tile(b * t, 1100)
    hin = _matmul([(h, 0, w_in)], tm, w_in.shape[1] // 3)
    q = _matmul([(hin, 0, w_uq)], tm, 1024, gain=e_g_q[0])
    kv = _matmul([(hin, kv_block, w_ukv)], tm, 1024, gain=e_g_kv[0])
    hin3 = hin.reshape(b, t, -1)
    attn = _mla_attention(q.reshape(b, t, -1), kv.reshape(b, t, -1), hin3, kr_block, mla_tabs, n_ctx)
    wb, wc, a_re, a_im = _s5_weights(e_s5_lam_re[0], e_s5_lam_im[0], e_s5_log_dt[0], e_s5_b_re[0],
                                     e_s5_b_im[0], e_s5_c_re[0], e_s5_c_im[0])
    y4 = _s5_scan(hin3, u_block, wb, wc, a_re, a_im, tr)
    s5 = _s5_glu(y4, hin3, u_block, e_s5_d[0], e_w_glu[0].astype(BF16), tr)
    y = _matmul([(attn.reshape(b * t, -1), 0, w_out[:n_attn]), (s5.reshape(b * t, -1), 0, w_out[n_attn:])],
                tm, 512)
    z, h = _residual((ctx, x), y.reshape(b, t, d), g_mix_post[0], modtabs[0], GT_M, 0, 0, tr,
                     nxt=(g_ffn_pre[0], modtabs[0], SH_F, SC_F))
    y = _ffn_matmuls(h, w_ff1, w_ff3, w_ff2b, 0)
    z, h = _residual(z, y, g_ffn_post[0], modtabs[0], GT_F, 0, 0, tr,
                     nxt=(g_mix_pre[1], modtabs[1], SH_M, SC_M))

    lam_init = 0.8 - 0.6 * math.exp(-0.3 * 1)
    w_in = o_w_in[0].astype(BF16)
    w_out = o_w_out[0].astype(BF16)
    n_attn = DIFF_HEADS * DIFF_V
    conv_ch = o_conv_w.shape[2]

    hin3 = _matmul([(h.reshape(b * t, d), 0, w_in)], tm, 512).reshape(b, t, -1)
    lam_vecs = jnp.stack([o_lam_q1[0], o_lam_k1[0], o_lam_q2[0], o_lam_k2[0]]).astype(F32)
    attn = _diff_attention(hin3, diff_tabs, lam_vecs, o_g_subln[0], lam_init, n_ctx)
    conv = _conformer_conv(hin3, 3 * n_attn // conv_ch, o_conv_w[0], o_conv_b[0], o_conv_norm_g[0],
                           o_conv_norm_b[0], n_ctx, tr)
    tml = _row_tile(b * n, 1100)
    y = _matmul([(attn.reshape(b * n, -1), 0, w_out[:n_attn]), (conv.reshape(b * n, -1), 0, w_out[n_attn:])],
                tml, 512)
    xl, h = _residual(z, y.reshape(b, n, d), g_mix_post[1], modtabs[1], GT_M, n_ctx // tr, 1, tr,
                      nxt=(g_ffn_pre[1], modtabs[1], SH_F, SC_F))
    y = _ffn_matmuls(h, w_ff1, w_ff3, w_ff2b, 1)
    return _residual(xl, y, g_ffn_post[1], modtabs[1], GT_F, 0, 1, tr)
```

```python
import functools
import math

import numpy as np

import jax
import jax.numpy as jnp
from jax import lax
from jax.experimental import pallas as pl
from jax.experimental.pallas import tpu as pltpu

F32 = jnp.float32
BF16 = jnp.bfloat16

EPS = 1e-6
ROPE_BASE = 10000.0
GRID_W = 64
LOG2E = math.log2(math.e)

MLA_HEADS = 16
MLA_NOPE = 128
MLA_ROPE = 64
MLA_V = 128
MLA_HEAD_PAD = 256
MLA_HEADS_PER_STEP = 2
DIFF_HEADS = 16
DIFF_QK = 64
DIFF_V = 128
S5_GROUP = 16
S5_GROUPS = 64
S5_STATE = 64
S5_BLOCKS = 8
N_MOD = 6

LANES = 128
SUBLANES = 8
VMEM_LIMIT = 56 * 1024 * 1024

SH_M, SC_M, GT_M, SH_F, SC_F, GT_F = range(N_MOD)


def _params(*sem):
    return pltpu.CompilerParams(dimension_semantics=sem, vmem_limit_bytes=VMEM_LIMIT)


def _row_tile(rows, cap):
    k = -(-rows // cap)
    while rows % k or (rows // k) % 16:
        k += 1
    return rows // k


def _mod_kernel(c_ref, w_ref, b_ref, o_ref):
    c = c_ref[...]
    a = (c * jax.nn.sigmoid(c)).astype(BF16)
    o_ref[...] = jnp.dot(a, w_ref[...].astype(BF16), preferred_element_type=F32) + b_ref[...]


def _modulation(cvec, w_mod, b_mod):
    depth, d, n = w_mod.shape
    tn = 512
    return pl.pallas_call(
        _mod_kernel,
        out_shape=jax.ShapeDtypeStruct((depth, SUBLANES, n), F32),
        grid=(depth, n // tn),
        in_specs=[pl.BlockSpec((SUBLANES, d), lambda l, j: (0, 0)),
                  pl.BlockSpec((None, d, tn), lambda l, j: (l, 0, j)),
                  pl.BlockSpec((None, 1, tn), lambda l, j: (l, 0, j))],
        out_specs=pl.BlockSpec((None, SUBLANES, tn), lambda l, j: (l, 0, j)),
        compiler_params=_params("parallel", "parallel"),
    )(cvec, w_mod, b_mod.reshape(depth, 1, n))


def _stream_specs(src, tr, d, tile_offset):
    if isinstance(src, tuple):
        return ([pl.BlockSpec((None, tr, d), lambda bi, i: (bi, 0, 0)),
                 pl.BlockSpec((None, tr, d), lambda bi, i: (bi, jnp.maximum(i - 1, 0), 0))], list(src))
    return [pl.BlockSpec((None, tr, d), lambda bi, i: (bi, i + tile_offset, 0))], [src]


def _stream_rows(x_refs):
    if len(x_refs) == 1:
        return x_refs[0][...]
    return jnp.where(pl.program_id(1) == 0, x_refs[0][...], x_refs[1][...])


def _mod_spec(d, seg_offset):
    return pl.BlockSpec((None, None, N_MOD, d), lambda bi, i: (bi, jnp.minimum(i + seg_offset, 1), 0, 0))


def _adaln(x, g_ref, m_ref, shift_row, scale_row):
    r = lax.rsqrt(jnp.mean(x * x, axis=-1, keepdims=True) + EPS)
    h = x * r * g_ref[...]
    return (h * (1.0 + m_ref[scale_row:scale_row + 1, :]) + m_ref[shift_row:shift_row + 1, :]).astype(BF16)


def _norm_mod_kernel(*refs, n_src, shift_row, scale_row):
    g_ref, m_ref, o_ref = refs[n_src:]
    o_ref[...] = _adaln(_stream_rows(refs[:n_src]), g_ref, m_ref, shift_row, scale_row)


def _norm_mod(src, gain, modtab, shift_row, scale_row, tr):
    specs, args = _stream_specs(src, tr, gain.shape[0], 0)
    b = args[0].shape[0]
    d = gain.shape[0]
    rows = sum(a.shape[1] for a in args)
    return pl.pallas_call(
        functools.partial(_norm_mod_kernel, n_src=len(args), shift_row=shift_row, scale_row=scale_row),
        out_shape=jax.ShapeDtypeStruct((b, rows, d), BF16),
        grid=(b, rows // tr),
        in_specs=specs + [pl.BlockSpec((1, d), lambda bi, i: (0, 0)), _mod_spec(d, 0)],
        out_specs=pl.BlockSpec((None, tr, d), lambda bi, i: (bi, i, 0)),
        compiler_params=_params("parallel", "parallel"),
    )(*args, gain.reshape(1, d), modtab)


def _residual_kernel(*refs, n_src, gate_row, nxt_rows):
    y_ref, g_ref, m_ref = refs[n_src:n_src + 3]
    y = y_ref[...].astype(F32)
    r = lax.rsqrt(jnp.mean(y * y, axis=-1, keepdims=True) + EPS)
    z = _stream_rows(refs[:n_src]) + m_ref[gate_row:gate_row + 1, :] * (y * r * g_ref[...])
    if nxt_rows is None:
        refs[-1][...] = z
    else:
        g2_ref, m2_ref, o_ref, h_ref = refs[n_src + 3:]
        o_ref[...] = z
        h_ref[...] = _adaln(z, g2_ref, m2_ref, *nxt_rows)


def _residual(src, y, gain, modtab, gate_row, x_tile_offset, seg_offset, tr, nxt=None):
    b, rows, d = y.shape
    specs, args = _stream_specs(src, tr, d, x_tile_offset)
    tile = lambda: pl.BlockSpec((None, tr, d), lambda bi, i: (bi, i, 0))
    vec = lambda: pl.BlockSpec((1, d), lambda bi, i: (0, 0))
    in_specs = specs + [tile(), vec(), _mod_spec(d, seg_offset)]
    args = args + [y, gain.reshape(1, d), modtab]
    out_shape = jax.ShapeDtypeStruct((b, rows, d), F32)
    out_specs = tile()
    nxt_rows = None
    if nxt is not None:
        in_specs += [vec(), _mod_spec(d, seg_offset)]
        args += [nxt[0].reshape(1, d), nxt[1]]
        nxt_rows = (nxt[2], nxt[3])
        out_shape = (out_shape, jax.ShapeDtypeStruct((b, rows, d), BF16))
        out_specs = (out_specs, tile())
    return pl.pallas_call(
        functools.partial(_residual_kernel, n_src=len(specs), gate_row=gate_row, nxt_rows=nxt_rows),
        out_shape=out_shape,
        grid=(b, rows // tr),
        in_specs=in_specs,
        out_specs=out_specs,
        compiler_params=_params("parallel", "parallel"),
    )(*args)


def _bf16(w):
    return w if w.dtype == BF16 else w.astype(BF16)


def _matmul_kernel(*refs, n_pairs, normed):
    o_ref = refs[-1]
    acc = None
    for p in range(n_pairs):
        a = refs[2 * p][...]
        if normed:
            af = a.astype(F32)
            r = lax.rsqrt(jnp.mean(af * af, axis=-1, keepdims=True) + EPS)
            a = (af * r * refs[2 * n_pairs][...]).astype(BF16)
        part = jnp.dot(a, _bf16(refs[2 * p + 1][...]), preferred_element_type=F32)
        acc = part if acc is None else acc + part
    o_ref[...] = acc.astype(o_ref.dtype)


def _weight_spec(w, k, row_block, layer, tn):
    if w.ndim == 3:
        return pl.BlockSpec((None, k, tn), lambda i, j: (layer, row_block, j))
    return pl.BlockSpec((k, tn), lambda i, j: (row_block, j))


def _matmul(pairs, tm, tn, gain=None, layer=0, out_dtype=BF16):
    rows = pairs[0][0].shape[0]
    n = pairs[0][2].shape[-1]
    tn = min(tn, n)
    in_specs, args = [], []
    for a, cb, w, rb, k in pairs:
        in_specs.append(pl.BlockSpec((tm, k), functools.partial(lambda i, j, cb: (i, cb), cb=cb)))
        in_specs.append(_weight_spec(w, k, rb, layer, tn))
        args += [a, w]
    if gain is not None:
        k = pairs[0][4]
        in_specs.append(pl.BlockSpec((1, k), lambda i, j: (0, 0)))
        args.append(gain.reshape(1, k).astype(F32))
    return pl.pallas_call(
        functools.partial(_matmul_kernel, n_pairs=len(pairs), normed=gain is not None),
        out_shape=jax.ShapeDtypeStruct((rows, n), out_dtype),
        grid=(rows // tm, n // tn),
        in_specs=in_specs,
        out_specs=pl.BlockSpec((tm, tn), lambda i, j: (i, j)),
        compiler_params=_params("parallel", "arbitrary"),
    )(*args)


def _swiglu_kernel(a_ref, w1_ref, w3_ref, w2_ref, o_ref, w2b_ref):
    a = a_ref[...]
    g = jnp.dot(a, _bf16(w1_ref[...]), preferred_element_type=F32)
    u = jnp.dot(a, _bf16(w3_ref[...]), preferred_element_type=F32)
    o_ref[...] = (g * jax.nn.sigmoid(g) * u).astype(o_ref.dtype)

    @pl.when(pl.program_id(0) == 0)
    def _():
        w2b_ref[...] = w2_ref[...].astype(BF16)


def _swiglu(a, w1, w3, w2, layer, tm, tn):
    rows, k = a.shape
    f = w1.shape[2]
    d = w2.shape[2]
    nj = f // tn

    def slab(i, j):
        return jnp.where(i == 0, j, nj - 1)

    return pl.pallas_call(
        _swiglu_kernel,
        out_shape=(jax.ShapeDtypeStruct((rows, f), BF16), jax.ShapeDtypeStruct((f, d), BF16)),
        grid=(rows // tm, nj),
        in_specs=[pl.BlockSpec((tm, k), lambda i, j: (i, 0)),
                  _weight_spec(w1, k, 0, layer, tn), _weight_spec(w3, k, 0, layer, tn),
                  pl.BlockSpec((None, tn, d), lambda i, j: (layer, slab(i, j), 0))],
        out_specs=(pl.BlockSpec((tm, tn), lambda i, j: (i, j)),
                   pl.BlockSpec((tn, d), lambda i, j: (slab(i, j), 0))),
        compiler_params=_params("arbitrary", "arbitrary"),
    )(a, w1, w3, w2)


def _rope_tables(n_tok, dim):
    rows = n_tok // GRID_W
    row = np.repeat(np.arange(rows, dtype=np.float32), GRID_W)
    col = np.tile(np.arange(GRID_W, dtype=np.float32), rows)
    axis_dim = dim // 2
    expo = -np.arange(0, axis_dim, 2, dtype=np.float32) / np.float32(axis_dim)
    inv = np.power(np.float32(ROPE_BASE), expo).astype(np.float32)
    ang_r = row[:, None] * inv[None, :]
    ang_c = col[:, None] * inv[None, :]
    ang = np.concatenate([ang_r, ang_r, ang_c, ang_c], axis=-1).astype(np.float32)
    return np.cos(ang), np.sin(ang)


def _rot_matrix(dim):
    q = dim // 4
    p = np.zeros((dim, dim), np.float32)
    eye = np.eye(q, dtype=np.float32)
    p[q:2 * q, 0:q] = -eye
    p[0:q, q:2 * q] = eye
    p[3 * q:, 2 * q:3 * q] = -eye
    p[2 * q:3 * q, 3 * q:] = eye
    return p


def _embed(block, shape, r0, c0, fill):
    out = np.full(shape, fill, np.float32)
    out[r0:r0 + block.shape[0], c0:c0 + block.shape[1]] = block
    return out


def _rope(x_bf16, cos, sin, rot):
    turned = jnp.dot(x_bf16, rot, preferred_element_type=F32)
    return x_bf16.astype(F32) * cos + turned * sin


KEY_CHUNK = 1024


def _softmax_values(scores, vones):
    keys = scores[0].shape[1]
    maxes = [jnp.max(s, axis=-1, keepdims=True) for s in scores]
    accs = [None] * len(scores)
    for c0 in range(0, keys, KEY_CHUNK):
        c1 = min(c0 + KEY_CHUNK, keys)
        for i, (s, m, vone) in enumerate(zip(scores, maxes, vones)):
            e = jnp.exp2(s[:, c0:c1] - m).astype(BF16)
            part = jnp.dot(e, vone[c0:c1, :], preferred_element_type=F32)
            accs[i] = part if accs[i] is None else accs[i] + part
    dv = accs[0].shape[1] // 2
    return [a[:, :dv] / a[:, dv:] for a in accs]


def _const_spec(shape):
    return pl.BlockSpec(shape, lambda bi, h, i: (0,) * len(shape), pipeline_mode=pl.Buffered(1))


_NT = (((1,), (1,)), ((), ()))


def _mla_kernel(q_ref, kv_ref, kr_ref, cq_ref, sq_ref, ck_ref, sk_ref, pq_ref, pk_ref,
                o_ref, kcat_ref, vone_ref, *, n_ctx, scale, heads):
    qi = pl.program_id(2)
    hp = MLA_HEAD_PAD

    @pl.when(qi == 0)
    def _():
        kr = _rope(kr_ref[...], ck_ref[...], sk_ref[...], pk_ref[...]).astype(BF16)
        ones = jnp.ones((kr.shape[0], hp - MLA_V), BF16)
        for h in range(heads):
            kcat_ref[h, :, :MLA_NOPE] = kv_ref[:, h * hp:h * hp + MLA_NOPE]
            kcat_ref[h, :, MLA_NOPE:] = kr
            vone_ref[h, :, :MLA_V] = kv_ref[:, h * hp + MLA_NOPE:(h + 1) * hp]
            vone_ref[h, :, MLA_V:] = ones

    def attend(rows):
        scores = []
        for h in range(heads):
            q = _rope(q_ref[:, h * hp:(h + 1) * hp], cq_ref[...], sq_ref[...], pq_ref[...]) * scale
            scores.append(lax.dot_general(q.astype(BF16), kcat_ref[h, :rows, :], _NT,
                                          preferred_element_type=F32))
        outs = _softmax_values(scores, [vone_ref.at[h, :rows, :] for h in range(heads)])
        for h in range(heads):
            o_ref[:, h * MLA_V:(h + 1) * MLA_V] = outs[h].astype(o_ref.dtype)

    @pl.when(qi == 0)
    def _():
        attend(n_ctx)

    @pl.when(qi > 0)
    def _():
        attend(kcat_ref.shape[1])


def _mla_attention(q, kv, hin, kr_block, tabs, n_ctx):
    b, t, _ = q.shape
    tq = n_ctx
    cq, sq, ck, sk, pq, pk = tabs
    scale = (MLA_NOPE + MLA_ROPE) ** -0.5 * LOG2E
    hp = MLA_HEAD_PAD
    hs = MLA_HEADS_PER_STEP
    assert hp == 2 * MLA_V and hp == MLA_NOPE + LANES
    return pl.pallas_call(
        functools.partial(_mla_kernel, n_ctx=n_ctx, scale=scale, heads=hs),
        out_shape=jax.ShapeDtypeStruct((b, t, MLA_HEADS * MLA_V), BF16),
        grid=(b, MLA_HEADS // hs, t // tq),
        in_specs=[pl.BlockSpec((None, tq, hs * hp), lambda bi, h, i: (bi, i, h)),
                  pl.BlockSpec((None, t, hs * hp), lambda bi, h, i: (bi, 0, h)),
                  pl.BlockSpec((None, t, LANES), lambda bi, h, i: (bi, 0, kr_block)),
                  pl.BlockSpec((tq, hp), lambda bi, h, i: (i, 0)),
                  pl.BlockSpec((tq, hp), lambda bi, h, i: (i, 0)),
                  _const_spec((t, LANES)), _const_spec((t, LANES)),
                  _const_spec((hp, hp)), _const_spec((LANES, LANES))],
        out_specs=pl.BlockSpec((None, tq, hs * MLA_V), lambda bi, h, i: (bi, i, h)),
        scratch_shapes=[pltpu.VMEM((hs, t, hp), BF16), pltpu.VMEM((hs, t, hp), BF16)],
        compiler_params=_params("parallel", "parallel", "arbitrary"),
    )(q, kv, hin, cq, sq, ck, sk, pq, pk)


def _diff_kernel(q_ref, k_ref, v_ref, cq_ref, sq_ref, ck_ref, sk_ref, p_ref, lam_ref, g_ref,
                 o_ref, kro_ref, vone_ref, *, scale, lam_init):
    qi = pl.program_id(2)

    @pl.when(qi == 0)
    def _():
        kro_ref[...] = _rope(k_ref[...], ck_ref[...], sk_ref[...], p_ref[...]).astype(BF16)
        vone_ref[:, :DIFF_V] = v_ref[...]
        vone_ref[:, DIFF_V:] = jnp.ones((v_ref.shape[0], DIFF_V), BF16)

    lv = lam_ref[...]
    lam = (jnp.exp(jnp.sum(lv[0:1] * lv[1:2], axis=-1, keepdims=True))
           - jnp.exp(jnp.sum(lv[2:3] * lv[3:4], axis=-1, keepdims=True)) + lam_init)

    q = _rope(q_ref[...], cq_ref[...], sq_ref[...], p_ref[...]) * scale
    lane = lax.broadcasted_iota(jnp.int32, q.shape, 1)

    def scores(qm):
        return lax.dot_general(qm.astype(BF16), kro_ref[...], _NT, preferred_element_type=F32)

    o1, o2 = _softmax_values([scores(jnp.where(lane < DIFF_QK, q, 0.0)),
                              scores(jnp.where(lane >= DIFF_QK, q, 0.0))], [vone_ref, vone_ref])
    o = o1 - lam * o2
    r = lax.rsqrt(jnp.mean(o * o, axis=-1, keepdims=True) + EPS)
    o_ref[...] = (o * r * g_ref[...] * (1.0 - lam_init)).astype(o_ref.dtype)


def _diff_attention(hin, tabs, lam_vecs, g_subln, lam_init, n_ctx):
    b, t, _ = hin.shape
    tq = n_ctx
    n = t - n_ctx
    cos, sin, p = tabs
    h_all = DIFF_HEADS
    return pl.pallas_call(
        functools.partial(_diff_kernel, scale=DIFF_QK ** -0.5 * LOG2E, lam_init=lam_init),
        out_shape=jax.ShapeDtypeStruct((b, n, h_all * DIFF_V), BF16),
        grid=(b, h_all, n // tq),
        in_specs=[pl.BlockSpec((None, tq, LANES), lambda bi, h, i: (bi, i + 1, h)),
                  pl.BlockSpec((None, t, LANES), lambda bi, h, i: (bi, 0, h_all + h)),
                  pl.BlockSpec((None, t, LANES), lambda bi, h, i: (bi, 0, 2 * h_all + h)),
                  pl.BlockSpec((tq, LANES), lambda bi, h, i: (i + 1, 0)),
                  pl.BlockSpec((tq, LANES), lambda bi, h, i: (i + 1, 0)),
                  _const_spec((t, LANES)), _const_spec((t, LANES)),
                  _const_spec((LANES, LANES)), _const_spec((4, DIFF_QK)), _const_spec((1, DIFF_V))],
        out_specs=pl.BlockSpec((None, tq, DIFF_V), lambda bi, h, i: (bi, i, h)),
        scratch_shapes=[pltpu.VMEM((t, LANES), BF16), pltpu.VMEM((t, 2 * DIFF_V), BF16)],
        compiler_params=_params("parallel", "parallel", "arbitrary"),
    )(hin, hin, hin, cos, sin, cos, sin, p, lam_vecs, g_subln.reshape(1, DIFF_V).astype(F32))


def _s5_kernel(uf_ref, ub_ref, wb_ref, wc_ref, are_ref, aim_ref, yf_ref, yb_ref, s_ref, hs_ref, h_ref,
               *, tc):
    half = S5_GROUPS // S5_BLOCKS * S5_STATE
    gw = S5_GROUPS // S5_BLOCKS * S5_GROUP
    nl = half // LANES
    u_refs = (uf_ref, ub_ref)
    y_refs = (yf_ref, yb_ref)

    @pl.when(pl.program_id(1) == 0)
    def _():
        h_ref[...] = jnp.zeros_like(h_ref)

    for d in range(2):
        for j in range(S5_BLOCKS):
            bu = jnp.dot(u_refs[d][:, j * gw:(j + 1) * gw], wb_ref[d, j], preferred_element_type=F32)
            for c in range(2 * nl):
                s_ref.at[d, c][pl.ds(j, tc, stride=S5_BLOCKS), :] = bu[:, c * LANES:(c + 1) * LANES]

    a_re = [[are_ref[d, :, c * LANES:(c + 1) * LANES] for c in range(nl)] for d in range(2)]
    a_im = [[aim_ref[d, :, c * LANES:(c + 1) * LANES] for c in range(nl)] for d in range(2)]

    def step(t, h):
        rows = (t * S5_BLOCKS, (tc - 1 - t) * S5_BLOCKS)
        out = []
        for d in range(2):
            row = pl.multiple_of(rows[d], S5_BLOCKS)
            new = [None] * (2 * nl)
            for c in range(nl):
                h_re, h_im = h[d * 2 * nl + c], h[d * 2 * nl + nl + c]
                new[c] = a_re[d][c] * h_re - a_im[d][c] * h_im + s_ref[d, c, pl.ds(row, S5_BLOCKS), :]
                new[nl + c] = (a_re[d][c] * h_im + a_im[d][c] * h_re
                               + s_ref[d, nl + c, pl.ds(row, S5_BLOCKS), :])
            for c in range(2 * nl):
                hs_ref[d, c, pl.ds(row, S5_BLOCKS), :] = new[c]
            out += new
        return tuple(out)

    init = tuple(h_ref[d, c] for d in range(2) for c in range(2 * nl))
    h = lax.fori_loop(0, tc, step, init, unroll=4)
    for d in range(2):
        for c in range(2 * nl):
            h_ref[d, c] = h[d * 2 * nl + c]

    for d in range(2):
        for j in range(S5_BLOCKS):
            hj = jnp.concatenate([hs_ref.at[d, c][pl.ds(j, tc, stride=S5_BLOCKS), :] for c in range(2 * nl)],
                                 axis=1).astype(BF16)
            y_refs[d][:, j * gw:(j + 1) * gw] = jnp.dot(hj, wc_ref[d, j], preferred_element_type=F32)


def _s5_scan(hin, u_block, wb, wc, a_re, a_im, tc):
    b, t, _ = hin.shape
    width = S5_GROUPS * S5_GROUP
    half = S5_GROUPS // S5_BLOCKS * S5_STATE
    nchunk = t // tc

    def back(k):
        return jnp.where(k == 0, 0, nchunk - k)

    full = lambda a: pl.BlockSpec(a.shape, lambda bi, k: (0,) * a.ndim)
    out = jax.ShapeDtypeStruct((b, t, width), F32)
    return pl.pallas_call(
        functools.partial(_s5_kernel, tc=tc),
        out_shape=(out, out),
        grid=(b, nchunk),
        in_specs=[pl.BlockSpec((None, tc, width), lambda bi, k: (bi, k, u_block)),
                  pl.BlockSpec((None, tc, width), lambda bi, k: (bi, back(k), u_block)),
                  full(wb), full(wc), full(a_re), full(a_im)],
        out_specs=(pl.BlockSpec((None, tc, width), lambda bi, k: (bi, k, 0)),
                   pl.BlockSpec((None, tc, width), lambda bi, k: (bi, back(k), 0))),
        scratch_shapes=[pltpu.VMEM((2, 2 * half // LANES, tc * S5_BLOCKS, LANES), F32),
                        pltpu.VMEM((2, 2 * half // LANES, tc * S5_BLOCKS, LANES), F32),
                        pltpu.VMEM((2, 2 * half // LANES, S5_BLOCKS, LANES), F32)],
        compiler_params=_params("parallel", "arbitrary"),
    )(hin, hin, wb, wc, a_re, a_im)


def _s5_weights(lam_re, lam_im, log_dt, b_re, b_im, c_re, c_im):
    lr = lam_re.astype(F32)
    li = lam_im.astype(F32)
    dt = jnp.exp(log_dt.astype(F32))[..., None]
    mag = jnp.exp(lr * dt)
    ab_re = mag * jnp.cos(li * dt)
    ab_im = mag * jnp.sin(li * dt)
    den = lr * lr + li * li
    num_re = ab_re - 1.0
    coef_re = ((num_re * lr + ab_im * li) / den)[..., None]
    coef_im = ((ab_im * lr - num_re * li) / den)[..., None]
    br = b_re.astype(F32)
    bi = b_im.astype(F32)
    bb_re = coef_re * br - coef_im * bi
    bb_im = coef_re * bi + coef_im * br
    nb = S5_BLOCKS
    gb = S5_GROUPS // nb
    eye = jnp.eye(gb, dtype=F32)

    def drive(bb):
        w = jnp.einsum('djgpi,gh->djgihp', bb.reshape(2, nb, gb, S5_STATE, S5_GROUP), eye)
        return w.reshape(2, nb, gb * S5_GROUP, gb * S5_STATE)

    def read(c):
        w = jnp.einsum('djgip,gh->djgphi', c.astype(F32).reshape(2, nb, gb, S5_GROUP, S5_STATE), eye)
        return w.reshape(2, nb, gb * S5_STATE, gb * S5_GROUP)

    wb = jnp.concatenate([drive(bb_re), drive(bb_im)], axis=-1).astype(BF16)
    wc = jnp.concatenate([read(c_re), -read(c_im)], axis=-2).astype(BF16)
    return wb, wc, ab_re.reshape(2, nb, gb * S5_STATE), ab_im.reshape(2, nb, gb * S5_STATE)


def _s5_glu_kernel(yf_ref, yb_ref, u_ref, d_ref, w_ref, o_ref):
    y = yf_ref[...] + yb_ref[...] + d_ref[...] * u_ref[...].astype(F32)
    r = jnp.dot(jax.nn.gelu(y).astype(BF16), w_ref[...], preferred_element_type=F32)
    n = r.shape[1] // 2
    o_ref[...] = (r[:, :n] * jax.nn.sigmoid(r[:, n:])).astype(o_ref.dtype)


def _s5_glu(yf, yb, hin, u_block, d_skip, w_glu, tm):
    b, t, width = yf.shape
    return pl.pallas_call(
        _s5_glu_kernel,
        out_shape=jax.ShapeDtypeStruct((b, t, width), BF16),
        grid=(b, t // tm),
        in_specs=[pl.BlockSpec((None, tm, width), lambda bi, i: (bi, i, 0)),
                  pl.BlockSpec((None, tm, width), lambda bi, i: (bi, i, 0)),
                  pl.BlockSpec((None, tm, width), lambda bi, i: (bi, i, u_block)),
                  pl.BlockSpec((1, width), lambda bi, i: (0, 0)),
                  pl.BlockSpec(w_glu.shape, lambda bi, i: (0, 0))],
        out_specs=pl.BlockSpec((None, tm, width), lambda bi, i: (bi, i, 0)),
        compiler_params=_params("parallel", "parallel"),
    )(yf, yb, hin, d_skip.reshape(1, width).astype(F32), w_glu)


CONV_HALO = 16


def _conv_kernel(ap_ref, gp_ref, ac_ref, gc_ref, an_ref, gn_ref, w_ref, b_ref, lg_ref, lb_ref,
                 o_ref, v_ref, sh_ref, y_ref, *, tr, taps):
    i = pl.program_id(1)
    last = pl.num_programs(1) - 1

    def glu(a_ref, g_ref):
        return a_ref[...].astype(F32) * jax.nn.sigmoid(g_ref[...].astype(F32))

    v_ref[0:CONV_HALO, :] = jnp.where(i > 0, glu(ap_ref, gp_ref), 0.0)
    v_ref[CONV_HALO:CONV_HALO + tr, :] = glu(ac_ref, gc_ref)
    v_ref[CONV_HALO + tr:, :] = jnp.where(i < last, glu(an_ref, gn_ref), 0.0)

    span = v_ref.shape[0] - SUBLANES
    for r in range(1, SUBLANES):
        sh_ref[r - 1, 0:span, :] = v_ref[r:r + span, :]

    first = CONV_HALO - taps // 2
    for c0 in range(0, v_ref.shape[1], LANES):
        acc = jnp.zeros((tr, LANES), F32)
        for k in range(taps):
            tile, r = divmod(first + k, SUBLANES)
            src = v_ref if r == 0 else sh_ref.at[r - 1]
            window = src[tile * SUBLANES:tile * SUBLANES + tr, c0:c0 + LANES]
            acc = acc + w_ref[k:k + 1, c0:c0 + LANES] * window
        y_ref[:, c0:c0 + LANES] = acc

    y = y_ref[...] + b_ref[...]
    yc = y - jnp.mean(y, axis=-1, keepdims=True)
    yn = yc * lax.rsqrt(jnp.mean(yc * yc, axis=-1, keepdims=True) + EPS) * lg_ref[...] + lb_ref[...]
    o_ref[...] = (yn * jax.nn.sigmoid(yn)).astype(o_ref.dtype)


def _conformer_conv(hin, a_block, w_dw, b_dw, g_n, b_n, n_ctx, tr):
    b, t, _ = hin.shape
    n = t - n_ctx
    taps, ch = w_dw.shape
    hb = tr // CONV_HALO
    off = n_ctx // tr
    last_halo = t // CONV_HALO - 1

    def prev_map(cb):
        return lambda bi, i: (bi, (i + off) * hb - 1, cb)

    def next_map(cb):
        return lambda bi, i: (bi, jnp.minimum((i + off + 1) * hb, last_halo), cb)

    def cur_map(cb):
        return lambda bi, i: (bi, i + off, cb)

    vec = lambda v: v.reshape(1, ch).astype(F32)
    halo = lambda m: pl.BlockSpec((None, CONV_HALO, ch), m)
    full = lambda shape: pl.BlockSpec(shape, lambda bi, i: (0, 0))
    return pl.pallas_call(
        functools.partial(_conv_kernel, tr=tr, taps=taps),
        out_shape=jax.ShapeDtypeStruct((b, n, ch), BF16),
        grid=(b, n // tr),
        in_specs=[halo(prev_map(a_block)), halo(prev_map(a_block + 1)),
                  pl.BlockSpec((None, tr, ch), cur_map(a_block)),
                  pl.BlockSpec((None, tr, ch), cur_map(a_block + 1)),
                  halo(next_map(a_block)), halo(next_map(a_block + 1)),
                  full((taps, ch)), full((1, ch)), full((1, ch)), full((1, ch))],
        out_specs=pl.BlockSpec((None, tr, ch), lambda bi, i: (bi, i, 0)),
        scratch_shapes=[pltpu.VMEM((tr + 2 * CONV_HALO, ch), F32),
                        pltpu.VMEM((SUBLANES - 1, tr + 2 * CONV_HALO, ch), F32),
                        pltpu.VMEM((tr, ch), F32)],
        compiler_params=_params("parallel", "parallel"),
    )(hin, hin, hin, hin, hin, hin, w_dw.astype(F32), vec(b_dw), vec(g_n), vec(b_n))


def _ffn_matmuls(h, w1, w3, w2, layer):
    b, rows, d = h.shape
    act, w2b = _swiglu(h.reshape(b * rows, d), w1, w3, w2, layer, _row_tile(b * rows, 1100), 256)
    y = _matmul([(act, 0, w2b, 0, w2b.shape[0])], _row_tile(b * rows, 550), 256)
    return y.reshape(b, rows, d)


def kernel(x, c, ctx, c_ctx, w_mod, b_mod, g_mix_pre, g_mix_post, g_ffn_pre, g_ffn_post, w_ff1, w_ff3, w_ff2, e_w_in, e_g_q, e_w_uq, e_g_kv, e_w_ukv, e_s5_lam_re, e_s5_lam_im, e_s5_log_dt, e_s5_b_re, e_s5_b_im, e_s5_c_re, e_s5_c_im, e_s5_d, e_w_glu, e_w_out, o_w_in, o_lam_q1, o_lam_k1, o_lam_q2, o_lam_k2, o_g_subln, o_conv_w, o_conv_b, o_conv_norm_g, o_conv_norm_b, o_w_out):
    b, n, d = x.shape
    n_ctx = ctx.shape[1]
    t = n_ctx + n
    depth = w_mod.shape[0]
    assert depth == 2 and e_w_in.shape[0] == 1 and o_w_in.shape[0] == 1
    assert b + 1 <= SUBLANES and n % n_ctx == 0
    tr = n_ctx
    q_rank = e_g_q.shape[1]
    kv_rank = e_g_kv.shape[1]
    s5_w = S5_GROUPS * S5_GROUP

    cvec = jnp.zeros((SUBLANES, d), F32).at[:b].set(c).at[b].set(c_ctx)
    mod = _modulation(cvec, w_mod, b_mod)
    modtabs = []
    for layer in range(depth):
        lat = mod[layer, :b].reshape(b, 1, N_MOD, d)
        cx = jnp.broadcast_to(mod[layer, b].reshape(1, 1, N_MOD, d), (b, 1, N_MOD, d))
        modtabs.append(jnp.concatenate([cx, lat], axis=1))

    cos, sin = _rope_tables(n, MLA_ROPE)
    rot = _rot_matrix(MLA_ROPE)
    hp = MLA_HEAD_PAD
    mla_tabs = (_embed(cos, (t, hp), n_ctx, MLA_NOPE, 1.0), _embed(sin, (t, hp), n_ctx, MLA_NOPE, 0.0),
                _embed(cos, (t, LANES), n_ctx, 0, 1.0), _embed(sin, (t, LANES), n_ctx, 0, 0.0),
                jnp.asarray(_embed(rot, (hp, hp), MLA_NOPE, MLA_NOPE, 0.0), BF16),
                jnp.asarray(_embed(rot, (LANES, LANES), 0, 0, 0.0), BF16))
    cos2 = np.concatenate([cos, cos], axis=1)
    sin2 = np.concatenate([sin, sin], axis=1)
    rot2 = _embed(rot, (LANES, LANES), 0, 0, 0.0)
    rot2[DIFF_QK:, DIFF_QK:] = rot
    diff_tabs = (_embed(cos2, (t, LANES), n_ctx, 0, 1.0), _embed(sin2, (t, LANES), n_ctx, 0, 0.0),
                 jnp.asarray(rot2, BF16))

    lw = e_w_in[0]
    cuts = [q_rank, q_rank + kv_rank, q_rank + kv_rank + MLA_ROPE]
    w_in = jnp.concatenate([lw[:, :cuts[0]], lw[:, cuts[2]:], lw[:, cuts[0]:cuts[1]], lw[:, cuts[1]:cuts[2]],
                            jnp.zeros((d, LANES - MLA_ROPE), F32)], axis=1).astype(BF16)
    u_block = q_rank // s5_w
    kv_block = (q_rank + s5_w) // kv_rank
    kr_block = (q_rank + s5_w + kv_rank) // LANES
    w_uq = e_w_uq[0].reshape(q_rank, MLA_HEADS, MLA_NOPE + MLA_ROPE)
    w_uq = jnp.pad(w_uq, ((0, 0), (0, 0), (0, hp - MLA_NOPE - MLA_ROPE))).reshape(q_rank, MLA_HEADS * hp)
    w_uq = w_uq.astype(BF16)
    n_attn = MLA_HEADS * MLA_V

    h = _norm_mod((ctx, x), g_mix_pre[0], modtabs[0], SH_M, SC_M, tr).reshape(b * t, d)
    tm = _row_tile(b * t, 1100)
    hin = _matmul([(h, 0, w_in, 0, d)], tm, w_in.shape[1] // 3)
    q = _matmul([(hin, 0, w_uq, 0, q_rank)], tm, 1024, gain=e_g_q[0])
    kv = _matmul([(hin, kv_block, e_w_ukv, 0, kv_rank)], tm, 1024, gain=e_g_kv[0])
    hin3 = hin.reshape(b, t, -1)
    attn = _mla_attention(q.reshape(b, t, -1), kv.reshape(b, t, -1), hin3, kr_block, mla_tabs, n_ctx)
    wb, wc, a_re, a_im = _s5_weights(e_s5_lam_re[0], e_s5_lam_im[0], e_s5_log_dt[0], e_s5_b_re[0],
                                     e_s5_b_im[0], e_s5_c_re[0], e_s5_c_im[0])
    yf, yb = _s5_scan(hin3, u_block, wb, wc, a_re, a_im, tr)
    s5 = _s5_glu(yf, yb, hin3, u_block, e_s5_d[0], e_w_glu[0].astype(BF16), tr)
    y = _matmul([(attn.reshape(b * t, -1), 0, e_w_out, 0, n_attn),
                 (s5.reshape(b * t, -1), 0, e_w_out, n_attn // s5_w, s5_w)], tm, 512)
    z, h = _residual((ctx, x), y.reshape(b, t, d), g_mix_post[0], modtabs[0], GT_M, 0, 0, tr,
                     nxt=(g_ffn_pre[0], modtabs[0], SH_F, SC_F))
    y = _ffn_matmuls(h, w_ff1, w_ff3, w_ff2, 0)
    z, h = _residual(z, y, g_ffn_post[0], modtabs[0], GT_F, 0, 0, tr,
                     nxt=(g_mix_pre[1], modtabs[1], SH_M, SC_M))

    lam_init = 0.8 - 0.6 * math.exp(-0.3 * 1)
    n_attn = DIFF_HEADS * DIFF_V
    conv_ch = o_conv_w.shape[2]

    hin3 = _matmul([(h.reshape(b * t, d), 0, o_w_in, 0, d)], tm, 512).reshape(b, t, -1)
    lam_vecs = jnp.stack([o_lam_q1[0], o_lam_k1[0], o_lam_q2[0], o_lam_k2[0]]).astype(F32)
    attn = _diff_attention(hin3, diff_tabs, lam_vecs, o_g_subln[0], lam_init, n_ctx)
    conv = _conformer_conv(hin3, 3 * n_attn // conv_ch, o_conv_w[0], o_conv_b[0], o_conv_norm_g[0],
                           o_conv_norm_b[0], n_ctx, tr)
    tml = _row_tile(b * n, 1100)
    y = _matmul([(attn.reshape(b * n, -1), 0, o_w_out, 0, n_attn),
                 (conv.reshape(b * n, -1), 0, o_w_out, n_attn // conv_ch, conv_ch)], tml, 512)
    xl, h = _residual(z, y.reshape(b, n, d), g_mix_post[1], modtabs[1], GT_M, n_ctx // tr, 1, tr,
                      nxt=(g_ffn_pre[1], modtabs[1], SH_F, SC_F))
    y = _ffn_matmuls(h, w_ff1, w_ff3, w_ff2, 1)
    return _residual(xl, y, g_ffn_post[1], modtabs[1], GT_F, 0, 1, tr)
```

```python
import functools
import math

import numpy as np

import jax
import jax.numpy as jnp
from jax import lax
from jax.experimental import pallas as pl
from jax.experimental.pallas import tpu as pltpu

F32 = jnp.float32
BF16 = jnp.bfloat16

EPS = 1e-6
ROPE_BASE = 10000.0
GRID_W = 64
LOG2E = math.log2(math.e)

MLA_HEADS = 16
MLA_NOPE = 128
MLA_ROPE = 64
MLA_V = 128
MLA_HEAD_PAD = 256
MLA_HEADS_PER_STEP = 2
DIFF_HEADS = 16
DIFF_QK = 64
DIFF_V = 128
S5_GROUP = 16
S5_GROUPS = 64
S5_STATE = 64
S5_BLOCKS = 8
N_MOD = 6

LANES = 128
SUBLANES = 8
VMEM_LIMIT = 56 * 1024 * 1024

SH_M, SC_M, GT_M, SH_F, SC_F, GT_F = range(N_MOD)


def _params(*sem):
    return pltpu.CompilerParams(dimension_semantics=sem, vmem_limit_bytes=VMEM_LIMIT)


def _row_tile(rows, cap):
    k = -(-rows // cap)
    while rows % k or (rows // k) % 16:
        k += 1
    return rows // k


def _mod_rows(c_ref, w_ref, b_ref):
    c = c_ref[...]
    a = (c * jax.nn.sigmoid(c)).astype(BF16)
    return jnp.dot(a, w_ref[...].astype(BF16), preferred_element_type=F32) + b_ref[...]


def _mod_kernel(c_ref, w_ref, b_ref, o_ref):
    o_ref[...] = _mod_rows(c_ref, w_ref, b_ref)


def _modulation(cvec, w_mod, b_mod, layers):
    _, d, n = w_mod.shape
    tn = 512
    return pl.pallas_call(
        _mod_kernel,
        out_shape=jax.ShapeDtypeStruct((layers, SUBLANES, n), F32),
        grid=(layers, n // tn),
        in_specs=[pl.BlockSpec((SUBLANES, d), lambda l, j: (0, 0)),
                  pl.BlockSpec((None, d, tn), lambda l, j: (l, 0, j)),
                  pl.BlockSpec((None, 1, tn), lambda l, j: (l, 0, j))],
        out_specs=pl.BlockSpec((None, SUBLANES, tn), lambda l, j: (l, 0, j)),
        compiler_params=_params("parallel", "parallel"),
    )(cvec, w_mod, b_mod)


def _stream_specs(src, tr, d, tile_offset):
    if isinstance(src, tuple):
        return ([pl.BlockSpec((None, tr, d), lambda bi, i: (bi, 0, 0)),
                 pl.BlockSpec((None, tr, d), lambda bi, i: (bi, jnp.maximum(i - 1, 0), 0))], list(src))
    return [pl.BlockSpec((None, tr, d), lambda bi, i: (bi, i + tile_offset, 0))], [src]


def _stream_rows(x_refs):
    if len(x_refs) == 1:
        return x_refs[0][...]
    return jnp.where(pl.program_id(1) == 0, x_refs[0][...], x_refs[1][...])


def _mod_spec(d, seg_offset):
    return pl.BlockSpec((None, None, N_MOD, d), lambda bi, i: (bi, jnp.minimum(i + seg_offset, 1), 0, 0))


def _adaln(x, g_ref, m_ref, shift_row, scale_row):
    r = lax.rsqrt(jnp.mean(x * x, axis=-1, keepdims=True) + EPS)
    h = x * r * g_ref[...]
    return (h * (1.0 + m_ref[scale_row:scale_row + 1, :]) + m_ref[shift_row:shift_row + 1, :]).astype(BF16)


def _norm_mod_kernel(*refs, n_src, shift_row, scale_row):
    g_ref, m_ref, o_ref = refs[n_src:]
    o_ref[...] = _adaln(_stream_rows(refs[:n_src]), g_ref, m_ref, shift_row, scale_row)


def _norm_mod(src, gain, modtab, shift_row, scale_row, tr):
    specs, args = _stream_specs(src, tr, gain.shape[0], 0)
    b = args[0].shape[0]
    d = gain.shape[0]
    rows = sum(a.shape[1] for a in args)
    return pl.pallas_call(
        functools.partial(_norm_mod_kernel, n_src=len(args), shift_row=shift_row, scale_row=scale_row),
        out_shape=jax.ShapeDtypeStruct((b, rows, d), BF16),
        grid=(b, rows // tr),
        in_specs=specs + [pl.BlockSpec((1, d), lambda bi, i: (0, 0)), _mod_spec(d, 0)],
        out_specs=pl.BlockSpec((None, tr, d), lambda bi, i: (bi, i, 0)),
        compiler_params=_params("parallel", "parallel"),
    )(*args, gain.reshape(1, d), modtab)


def _residual_kernel(*refs, n_src, gate_row, nxt_rows):
    y_ref, g_ref, m_ref = refs[n_src:n_src + 3]
    y = y_ref[...].astype(F32)
    r = lax.rsqrt(jnp.mean(y * y, axis=-1, keepdims=True) + EPS)
    z = _stream_rows(refs[:n_src]) + m_ref[gate_row:gate_row + 1, :] * (y * r * g_ref[...])
    if nxt_rows is None:
        refs[-1][...] = z
    else:
        g2_ref, m2_ref, o_ref, h_ref = refs[n_src + 3:]
        o_ref[...] = z
        h_ref[...] = _adaln(z, g2_ref, m2_ref, *nxt_rows)


def _residual(src, y, gain, modtab, gate_row, x_tile_offset, seg_offset, tr, nxt=None):
    b, rows, d = y.shape
    specs, args = _stream_specs(src, tr, d, x_tile_offset)
    tile = lambda: pl.BlockSpec((None, tr, d), lambda bi, i: (bi, i, 0))
    vec = lambda: pl.BlockSpec((1, d), lambda bi, i: (0, 0))
    in_specs = specs + [tile(), vec(), _mod_spec(d, seg_offset)]
    args = args + [y, gain.reshape(1, d), modtab]
    out_shape = jax.ShapeDtypeStruct((b, rows, d), F32)
    out_specs = tile()
    nxt_rows = None
    if nxt is not None:
        in_specs += [vec(), _mod_spec(d, seg_offset)]
        args += [nxt[0].reshape(1, d), nxt[1]]
        nxt_rows = (nxt[2], nxt[3])
        out_shape = (out_shape, jax.ShapeDtypeStruct((b, rows, d), BF16))
        out_specs = (out_specs, tile())
    return pl.pallas_call(
        functools.partial(_residual_kernel, n_src=len(specs), gate_row=gate_row, nxt_rows=nxt_rows),
        out_shape=out_shape,
        grid=(b, rows // tr),
        in_specs=in_specs,
        out_specs=out_specs,
        compiler_params=_params("parallel", "parallel"),
    )(*args)


def _bf16(w):
    return w if w.dtype == BF16 else w.astype(BF16)


def _matmul_kernel(*refs, n_pairs, normed):
    o_ref = refs[-1]
    acc = None
    for p in range(n_pairs):
        a = refs[2 * p][...]
        if normed:
            af = a.astype(F32)
            r = lax.rsqrt(jnp.mean(af * af, axis=-1, keepdims=True) + EPS)
            a = (af * r * refs[2 * n_pairs][...]).astype(BF16)
        part = jnp.dot(a, _bf16(refs[2 * p + 1][...]), preferred_element_type=F32)
        acc = part if acc is None else acc + part
    o_ref[...] = acc.astype(o_ref.dtype)


def _weight_spec(w, k, row_block, layer, tn):
    if w.ndim == 3:
        return pl.BlockSpec((None, k, tn), lambda i, j: (layer, row_block, j))
    return pl.BlockSpec((k, tn), lambda i, j: (row_block, j))


def _matmul(pairs, tm, tn, gain=None, layer=0, out_dtype=BF16):
    rows = pairs[0][0].shape[0]
    n = pairs[0][2].shape[-1]
    tn = min(tn, n)
    in_specs, args = [], []
    for a, cb, w, rb, k in pairs:
        in_specs.append(pl.BlockSpec((tm, k), functools.partial(lambda i, j, cb: (i, cb), cb=cb)))
        in_specs.append(_weight_spec(w, k, rb, layer, tn))
        args += [a, w]
    if gain is not None:
        k = pairs[0][4]
        in_specs.append(pl.BlockSpec((1, k), lambda i, j: (0, 0)))
        args.append(gain.reshape(1, k).astype(F32))
    return pl.pallas_call(
        functools.partial(_matmul_kernel, n_pairs=len(pairs), normed=gain is not None),
        out_shape=jax.ShapeDtypeStruct((rows, n), out_dtype),
        grid=(rows // tm, n // tn),
        in_specs=in_specs,
        out_specs=pl.BlockSpec((tm, tn), lambda i, j: (i, j)),
        compiler_params=_params("parallel", "arbitrary"),
    )(*args)


def _swiglu_kernel(*refs, mod_tiles):
    a_ref, w1_ref, w3_ref, w2_ref = refs[:4]
    o_ref, w2b_ref = refs[-2 - (mod_tiles > 0):][:2]
    a = a_ref[...]
    g = jnp.dot(a, _bf16(w1_ref[...]), preferred_element_type=F32)
    u = jnp.dot(a, _bf16(w3_ref[...]), preferred_element_type=F32)
    o_ref[...] = (g * jax.nn.sigmoid(g) * u).astype(o_ref.dtype)

    @pl.when(pl.program_id(0) == 0)
    def _():
        w2b_ref[...] = w2_ref[...].astype(BF16)

    if mod_tiles:
        c_ref, wm_ref, bm_ref = refs[4:7]

        @pl.when(pl.program_id(0) * pl.num_programs(1) + pl.program_id(1) < mod_tiles)
        def _():
            refs[-1][...] = _mod_rows(c_ref, wm_ref, bm_ref)


def _swiglu(a, w1, w3, w2, layer, tm, tn, mod=None):
    rows, k = a.shape
    f = w1.shape[2]
    d = w2.shape[2]
    ni, nj = rows // tm, f // tn

    def slab(i, j):
        return jnp.where(i == 0, j, nj - 1)

    in_specs = [pl.BlockSpec((tm, k), lambda i, j: (i, 0)),
                _weight_spec(w1, k, 0, layer, tn), _weight_spec(w3, k, 0, layer, tn),
                pl.BlockSpec((None, tn, d), lambda i, j: (layer, slab(i, j), 0))]
    out_shape = [jax.ShapeDtypeStruct((rows, f), BF16), jax.ShapeDtypeStruct((f, d), BF16)]
    out_specs = [pl.BlockSpec((tm, tn), lambda i, j: (i, j)),
                 pl.BlockSpec((tn, d), lambda i, j: (slab(i, j), 0))]
    args = [a, w1, w3, w2]
    mod_tiles = 0
    if mod is not None:
        cvec, w_mod, b_mod, ml = mod
        n_mod = w_mod.shape[2]
        per_step = -(-n_mod // (LANES * ni * nj))
        while (n_mod // LANES) % per_step:
            per_step += 1
        mod_tile = per_step * LANES
        mod_tiles = n_mod // mod_tile

        def tile(i, j):
            return jnp.minimum(i * nj + j, mod_tiles - 1)

        in_specs += [pl.BlockSpec(cvec.shape, lambda i, j: (0, 0)),
                     pl.BlockSpec((None, w_mod.shape[1], mod_tile), lambda i, j: (ml, 0, tile(i, j))),
                     pl.BlockSpec((None, 1, mod_tile), lambda i, j: (ml, 0, tile(i, j)))]
        out_shape.append(jax.ShapeDtypeStruct((SUBLANES, n_mod), F32))
        out_specs.append(pl.BlockSpec((SUBLANES, mod_tile), lambda i, j: (0, tile(i, j))))
        args += [cvec, w_mod, b_mod]
    return pl.pallas_call(
        functools.partial(_swiglu_kernel, mod_tiles=mod_tiles),
        out_shape=tuple(out_shape),
        grid=(ni, nj),
        in_specs=in_specs,
        out_specs=tuple(out_specs),
        compiler_params=_params("arbitrary", "arbitrary"),
    )(*args)


def _rope_tables(n_tok, dim):
    rows = n_tok // GRID_W
    row = np.repeat(np.arange(rows, dtype=np.float32), GRID_W)
    col = np.tile(np.arange(GRID_W, dtype=np.float32), rows)
    axis_dim = dim // 2
    expo = -np.arange(0, axis_dim, 2, dtype=np.float32) / np.float32(axis_dim)
    inv = np.power(np.float32(ROPE_BASE), expo).astype(np.float32)
    ang_r = row[:, None] * inv[None, :]
    ang_c = col[:, None] * inv[None, :]
    ang = np.concatenate([ang_r, ang_r, ang_c, ang_c], axis=-1).astype(np.float32)
    return np.cos(ang), np.sin(ang)


def _rot_matrix(dim):
    q = dim // 4
    p = np.zeros((dim, dim), np.float32)
    eye = np.eye(q, dtype=np.float32)
    p[q:2 * q, 0:q] = -eye
    p[0:q, q:2 * q] = eye
    p[3 * q:, 2 * q:3 * q] = -eye
    p[2 * q:3 * q, 3 * q:] = eye
    return p


def _embed(block, shape, r0, c0, fill):
    out = np.full(shape, fill, np.float32)
    out[r0:r0 + block.shape[0], c0:c0 + block.shape[1]] = block
    return out


def _rope(x_bf16, cos, sin, rot):
    turned = jnp.dot(x_bf16, rot, preferred_element_type=F32)
    return x_bf16.astype(F32) * cos + turned * sin


KEY_CHUNK = 1024


def _softmax_values(scores, vones):
    keys = scores[0].shape[1]
    maxes = [jnp.max(s, axis=-1, keepdims=True) for s in scores]
    accs = [None] * len(scores)
    for c0 in range(0, keys, KEY_CHUNK):
        c1 = min(c0 + KEY_CHUNK, keys)
        for i, (s, m, vone) in enumerate(zip(scores, maxes, vones)):
            e = jnp.exp2(s[:, c0:c1] - m).astype(BF16)
            part = jnp.dot(e, vone[c0:c1, :], preferred_element_type=F32)
            accs[i] = part if accs[i] is None else accs[i] + part
    dv = accs[0].shape[1] // 2
    return [a[:, :dv] / a[:, dv:] for a in accs]


def _const_spec(shape):
    return pl.BlockSpec(shape, lambda bi, h, i: (0,) * len(shape), pipeline_mode=pl.Buffered(1))


_NT = (((1,), (1,)), ((), ()))


def _mla_kernel(q_ref, kv_ref, kr_ref, tq_ref, ck_ref, sk_ref, pk_ref, o_ref, kcat_ref, vone_ref,
                *, n_ctx, scale, heads):
    qi = pl.program_id(2)
    hp = MLA_HEAD_PAD

    @pl.when(qi == 0)
    def _():
        kr = _rope(kr_ref[...], ck_ref[...], sk_ref[...], pk_ref[...]).astype(BF16)
        ones = jnp.ones((kr.shape[0], hp - MLA_V), BF16)
        for h in range(heads):
            kcat_ref[h, :, :MLA_NOPE] = kv_ref[:, h * hp:h * hp + MLA_NOPE]
            kcat_ref[h, :, MLA_NOPE:] = kr
            vone_ref[h, :, :MLA_V] = kv_ref[:, h * hp + MLA_NOPE:(h + 1) * hp]
            vone_ref[h, :, MLA_V:] = ones

    def attend(rows):
        scores = []
        for h in range(heads):
            q = q_ref[:, h * hp:(h + 1) * hp].astype(F32) * tq_ref[...] * scale
            scores.append(lax.dot_general(q.astype(BF16), kcat_ref[h, :rows, :], _NT,
                                          preferred_element_type=F32))
        outs = _softmax_values(scores, [vone_ref.at[h, :rows, :] for h in range(heads)])
        for h in range(heads):
            o_ref[:, h * MLA_V:(h + 1) * MLA_V] = outs[h].astype(o_ref.dtype)

    @pl.when(qi == 0)
    def _():
        attend(n_ctx)

    @pl.when(qi > 0)
    def _():
        attend(kcat_ref.shape[1])


def _mla_attention(q, kv, hin, kr_block, tabs, n_ctx):
    b, t, _ = q.shape
    tq = n_ctx
    tq_tab, ck, sk, pk = tabs
    scale = (MLA_NOPE + MLA_ROPE) ** -0.5 * LOG2E
    hp = MLA_HEAD_PAD
    hs = MLA_HEADS_PER_STEP
    assert hp == 2 * MLA_V and hp == MLA_NOPE + 2 * MLA_ROPE == MLA_NOPE + LANES
    return pl.pallas_call(
        functools.partial(_mla_kernel, n_ctx=n_ctx, scale=scale, heads=hs),
        out_shape=jax.ShapeDtypeStruct((b, t, MLA_HEADS * MLA_V), BF16),
        grid=(b, MLA_HEADS // hs, t // tq),
        in_specs=[pl.BlockSpec((None, tq, hs * hp), lambda bi, h, i: (bi, i, h)),
                  pl.BlockSpec((None, t, hs * hp), lambda bi, h, i: (bi, 0, h)),
                  pl.BlockSpec((None, t, LANES), lambda bi, h, i: (bi, 0, kr_block)),
                  pl.BlockSpec((tq, hp), lambda bi, h, i: (i, 0)),
                  _const_spec((t, LANES)), _const_spec((t, LANES)), _const_spec((LANES, LANES))],
        out_specs=pl.BlockSpec((None, tq, hs * MLA_V), lambda bi, h, i: (bi, i, h)),
        scratch_shapes=[pltpu.VMEM((hs, t, hp), BF16), pltpu.VMEM((hs, t, hp), BF16)],
        compiler_params=_params("parallel", "parallel", "arbitrary"),
    )(q, kv, hin, tq_tab, ck, sk, pk)


def _diff_kernel(q_ref, k_ref, v_ref, cq_ref, sq_ref, ck_ref, sk_ref, p_ref, lam_ref, g_ref,
                 o_ref, kro_ref, vone_ref, *, scale, lam_init):
    qi = pl.program_id(2)

    @pl.when(qi == 0)
    def _():
        kro_ref[...] = _rope(k_ref[...], ck_ref[...], sk_ref[...], p_ref[...]).astype(BF16)
        vone_ref[:, :DIFF_V] = v_ref[...]
        vone_ref[:, DIFF_V:] = jnp.ones((v_ref.shape[0], DIFF_V), BF16)

    lv = lam_ref[...]
    lam = (jnp.exp(jnp.sum(lv[0:1] * lv[1:2], axis=-1, keepdims=True))
           - jnp.exp(jnp.sum(lv[2:3] * lv[3:4], axis=-1, keepdims=True)) + lam_init)

    q = _rope(q_ref[...], cq_ref[...], sq_ref[...], p_ref[...]) * scale
    lane = lax.broadcasted_iota(jnp.int32, q.shape, 1)

    def scores(qm):
        return lax.dot_general(qm.astype(BF16), kro_ref[...], _NT, preferred_element_type=F32)

    o1, o2 = _softmax_values([scores(jnp.where(lane < DIFF_QK, q, 0.0)),
                              scores(jnp.where(lane >= DIFF_QK, q, 0.0))], [vone_ref, vone_ref])
    o = o1 - lam * o2
    r = lax.rsqrt(jnp.mean(o * o, axis=-1, keepdims=True) + EPS)
    o_ref[...] = (o * r * g_ref[...] * (1.0 - lam_init)).astype(o_ref.dtype)


def _diff_attention(hin, tabs, lam_vecs, g_subln, lam_init, n_ctx):
    b, t, _ = hin.shape
    tq = n_ctx
    n = t - n_ctx
    cos, sin, p = tabs
    h_all = DIFF_HEADS
    return pl.pallas_call(
        functools.partial(_diff_kernel, scale=DIFF_QK ** -0.5 * LOG2E, lam_init=lam_init),
        out_shape=jax.ShapeDtypeStruct((b, n, h_all * DIFF_V), BF16),
        grid=(b, h_all, n // tq),
        in_specs=[pl.BlockSpec((None, tq, LANES), lambda bi, h, i: (bi, i + 1, h)),
                  pl.BlockSpec((None, t, LANES), lambda bi, h, i: (bi, 0, h_all + h)),
                  pl.BlockSpec((None, t, LANES), lambda bi, h, i: (bi, 0, 2 * h_all + h)),
                  pl.BlockSpec((tq, LANES), lambda bi, h, i: (i + 1, 0)),
                  pl.BlockSpec((tq, LANES), lambda bi, h, i: (i + 1, 0)),
                  _const_spec((t, LANES)), _const_spec((t, LANES)),
                  _const_spec((LANES, LANES)), _const_spec((4, DIFF_QK)), _const_spec((1, DIFF_V))],
        out_specs=pl.BlockSpec((None, tq, DIFF_V), lambda bi, h, i: (bi, i, h)),
        scratch_shapes=[pltpu.VMEM((t, LANES), BF16), pltpu.VMEM((t, 2 * DIFF_V), BF16)],
        compiler_params=_params("parallel", "parallel", "arbitrary"),
    )(hin, hin, hin, cos, sin, cos, sin, p, lam_vecs, g_subln.reshape(1, DIFF_V).astype(F32))


def _s5_kernel(uf_ref, ub_ref, wb_ref, wc_ref, are_ref, aim_ref, yf_ref, yb_ref, s_ref, hs_ref, h_ref,
               *, tc):
    half = S5_GROUPS // S5_BLOCKS * S5_STATE
    gw = S5_GROUPS // S5_BLOCKS * S5_GROUP
    nl = half // LANES
    u_refs = (uf_ref, ub_ref)
    y_refs = (yf_ref, yb_ref)

    @pl.when(pl.program_id(1) == 0)
    def _():
        h_ref[...] = jnp.zeros_like(h_ref)

    for d in range(2):
        for j in range(S5_BLOCKS):
            bu = jnp.dot(u_refs[d][:, j * gw:(j + 1) * gw], wb_ref[d, j], preferred_element_type=F32)
            for c in range(2 * nl):
                s_ref.at[d, c][pl.ds(j, tc, stride=S5_BLOCKS), :] = bu[:, c * LANES:(c + 1) * LANES]

    a_re = [[are_ref[d, :, c * LANES:(c + 1) * LANES] for c in range(nl)] for d in range(2)]
    a_im = [[aim_ref[d, :, c * LANES:(c + 1) * LANES] for c in range(nl)] for d in range(2)]

    def step(t, h):
        rows = (t * S5_BLOCKS, (tc - 1 - t) * S5_BLOCKS)
        out = []
        for d in range(2):
            row = pl.multiple_of(rows[d], S5_BLOCKS)
            new = [None] * (2 * nl)
            for c in range(nl):
                h_re, h_im = h[d * 2 * nl + c], h[d * 2 * nl + nl + c]
                new[c] = a_re[d][c] * h_re - a_im[d][c] * h_im + s_ref[d, c, pl.ds(row, S5_BLOCKS), :]
                new[nl + c] = (a_re[d][c] * h_im + a_im[d][c] * h_re
                               + s_ref[d, nl + c, pl.ds(row, S5_BLOCKS), :])
            for c in range(2 * nl):
                hs_ref[d, c, pl.ds(row, S5_BLOCKS), :] = new[c]
            out += new
        return tuple(out)

    init = tuple(h_ref[d, c] for d in range(2) for c in range(2 * nl))
    h = lax.fori_loop(0, tc, step, init, unroll=4)
    for d in range(2):
        for c in range(2 * nl):
            h_ref[d, c] = h[d * 2 * nl + c]

    for d in range(2):
        for j in range(S5_BLOCKS):
            hj = jnp.concatenate([hs_ref.at[d, c][pl.ds(j, tc, stride=S5_BLOCKS), :] for c in range(2 * nl)],
                                 axis=1).astype(BF16)
            y_refs[d][:, j * gw:(j + 1) * gw] = jnp.dot(hj, wc_ref[d, j], preferred_element_type=F32)


def _s5_scan(hin, u_block, wb, wc, a_re, a_im, tc):
    b, t, _ = hin.shape
    width = S5_GROUPS * S5_GROUP
    half = S5_GROUPS // S5_BLOCKS * S5_STATE
    nchunk = t // tc

    def back(k):
        return jnp.where(k == 0, 0, nchunk - k)

    full = lambda a: pl.BlockSpec(a.shape, lambda bi, k: (0,) * a.ndim)
    out = jax.ShapeDtypeStruct((b, t, width), F32)
    return pl.pallas_call(
        functools.partial(_s5_kernel, tc=tc),
        out_shape=(out, out),
        grid=(b, nchunk),
        in_specs=[pl.BlockSpec((None, tc, width), lambda bi, k: (bi, k, u_block)),
                  pl.BlockSpec((None, tc, width), lambda bi, k: (bi, back(k), u_block)),
                  full(wb), full(wc), full(a_re), full(a_im)],
        out_specs=(pl.BlockSpec((None, tc, width), lambda bi, k: (bi, k, 0)),
                   pl.BlockSpec((None, tc, width), lambda bi, k: (bi, back(k), 0))),
        scratch_shapes=[pltpu.VMEM((2, 2 * half // LANES, tc * S5_BLOCKS, LANES), F32),
                        pltpu.VMEM((2, 2 * half // LANES, tc * S5_BLOCKS, LANES), F32),
                        pltpu.VMEM((2, 2 * half // LANES, S5_BLOCKS, LANES), F32)],
        compiler_params=_params("parallel", "arbitrary"),
    )(hin, hin, wb, wc, a_re, a_im)


def _s5_weights(lam_re, lam_im, log_dt, b_re, b_im, c_re, c_im):
    lr = lam_re.astype(F32)
    li = lam_im.astype(F32)
    dt = jnp.exp(log_dt.astype(F32))[..., None]
    mag = jnp.exp(lr * dt)
    ab_re = mag * jnp.cos(li * dt)
    ab_im = mag * jnp.sin(li * dt)
    den = lr * lr + li * li
    num_re = ab_re - 1.0
    coef_re = ((num_re * lr + ab_im * li) / den)[..., None]
    coef_im = ((ab_im * lr - num_re * li) / den)[..., None]
    br = b_re.astype(F32)
    bi = b_im.astype(F32)
    bb_re = coef_re * br - coef_im * bi
    bb_im = coef_re * bi + coef_im * br
    nb = S5_BLOCKS
    gb = S5_GROUPS // nb
    eye = jnp.eye(gb, dtype=F32)

    def drive(bb):
        w = jnp.einsum('djgpi,gh->djgihp', bb.reshape(2, nb, gb, S5_STATE, S5_GROUP), eye)
        return w.reshape(2, nb, gb * S5_GROUP, gb * S5_STATE)

    def read(c):
        w = jnp.einsum('djgip,gh->djgphi', c.astype(F32).reshape(2, nb, gb, S5_GROUP, S5_STATE), eye)
        return w.reshape(2, nb, gb * S5_STATE, gb * S5_GROUP)

    wb = jnp.concatenate([drive(bb_re), drive(bb_im)], axis=-1).astype(BF16)
    wc = jnp.concatenate([read(c_re), -read(c_im)], axis=-2).astype(BF16)
    return wb, wc, ab_re.reshape(2, nb, gb * S5_STATE), ab_im.reshape(2, nb, gb * S5_STATE)


def _s5_glu_kernel(yf_ref, yb_ref, u_ref, d_ref, w_ref, o_ref):
    y = yf_ref[...] + yb_ref[...] + d_ref[...] * u_ref[...].astype(F32)
    r = jnp.dot(jax.nn.gelu(y).astype(BF16), w_ref[...], preferred_element_type=F32)
    n = r.shape[1] // 2
    o_ref[...] = (r[:, :n] * jax.nn.sigmoid(r[:, n:])).astype(o_ref.dtype)


def _s5_glu(yf, yb, hin, u_block, d_skip, w_glu, tm):
    b, t, width = yf.shape
    return pl.pallas_call(
        _s5_glu_kernel,
        out_shape=jax.ShapeDtypeStruct((b, t, width), BF16),
        grid=(b, t // tm),
        in_specs=[pl.BlockSpec((None, tm, width), lambda bi, i: (bi, i, 0)),
                  pl.BlockSpec((None, tm, width), lambda bi, i: (bi, i, 0)),
                  pl.BlockSpec((None, tm, width), lambda bi, i: (bi, i, u_block)),
                  pl.BlockSpec((1, width), lambda bi, i: (0, 0)),
                  pl.BlockSpec(w_glu.shape, lambda bi, i: (0, 0))],
        out_specs=pl.BlockSpec((None, tm, width), lambda bi, i: (bi, i, 0)),
        compiler_params=_params("parallel", "parallel"),
    )(yf, yb, hin, d_skip.reshape(1, width).astype(F32), w_glu)


CONV_HALO = 16


def _conv_kernel(ap_ref, gp_ref, ac_ref, gc_ref, an_ref, gn_ref, w_ref, b_ref, lg_ref, lb_ref,
                 o_ref, v_ref, sh_ref, y_ref, *, tr, taps):
    i = pl.program_id(1)
    last = pl.num_programs(1) - 1

    def glu(a_ref, g_ref):
        return a_ref[...].astype(F32) * jax.nn.sigmoid(g_ref[...].astype(F32))

    v_ref[0:CONV_HALO, :] = jnp.where(i > 0, glu(ap_ref, gp_ref), 0.0)
    v_ref[CONV_HALO:CONV_HALO + tr, :] = glu(ac_ref, gc_ref)
    v_ref[CONV_HALO + tr:, :] = jnp.where(i < last, glu(an_ref, gn_ref), 0.0)

    span = v_ref.shape[0] - SUBLANES
    for r in range(1, SUBLANES):
        sh_ref[r - 1, 0:span, :] = v_ref[r:r + span, :]

    first = CONV_HALO - taps // 2
    for c0 in range(0, v_ref.shape[1], LANES):
        acc = jnp.zeros((tr, LANES), F32)
        for k in range(taps):
            tile, r = divmod(first + k, SUBLANES)
            src = v_ref if r == 0 else sh_ref.at[r - 1]
            window = src[tile * SUBLANES:tile * SUBLANES + tr, c0:c0 + LANES]
            acc = acc + w_ref[k:k + 1, c0:c0 + LANES] * window
        y_ref[:, c0:c0 + LANES] = acc

    y = y_ref[...] + b_ref[...]
    yc = y - jnp.mean(y, axis=-1, keepdims=True)
    yn = yc * lax.rsqrt(jnp.mean(yc * yc, axis=-1, keepdims=True) + EPS) * lg_ref[...] + lb_ref[...]
    o_ref[...] = (yn * jax.nn.sigmoid(yn)).astype(o_ref.dtype)


def _conformer_conv(hin, a_block, w_dw, b_dw, g_n, b_n, n_ctx, tr):
    b, t, _ = hin.shape
    n = t - n_ctx
    taps, ch = w_dw.shape
    hb = tr // CONV_HALO
    off = n_ctx // tr
    last_halo = t // CONV_HALO - 1

    def prev_map(cb):
        return lambda bi, i: (bi, (i + off) * hb - 1, cb)

    def next_map(cb):
        return lambda bi, i: (bi, jnp.minimum((i + off + 1) * hb, last_halo), cb)

    def cur_map(cb):
        return lambda bi, i: (bi, i + off, cb)

    vec = lambda v: v.reshape(1, ch).astype(F32)
    halo = lambda m: pl.BlockSpec((None, CONV_HALO, ch), m)
    full = lambda shape: pl.BlockSpec(shape, lambda bi, i: (0, 0))
    return pl.pallas_call(
        functools.partial(_conv_kernel, tr=tr, taps=taps),
        out_shape=jax.ShapeDtypeStruct((b, n, ch), BF16),
        grid=(b, n // tr),
        in_specs=[halo(prev_map(a_block)), halo(prev_map(a_block + 1)),
                  pl.BlockSpec((None, tr, ch), cur_map(a_block)),
                  pl.BlockSpec((None, tr, ch), cur_map(a_block + 1)),
                  halo(next_map(a_block)), halo(next_map(a_block + 1)),
                  full((taps, ch)), full((1, ch)), full((1, ch)), full((1, ch))],
        out_specs=pl.BlockSpec((None, tr, ch), lambda bi, i: (bi, i, 0)),
        scratch_shapes=[pltpu.VMEM((tr + 2 * CONV_HALO, ch), F32),
                        pltpu.VMEM((SUBLANES - 1, tr + 2 * CONV_HALO, ch), F32),
                        pltpu.VMEM((tr, ch), F32)],
        compiler_params=_params("parallel", "parallel"),
    )(hin, hin, hin, hin, hin, hin, w_dw.astype(F32), vec(b_dw), vec(g_n), vec(b_n))


def _ffn_matmuls(h, w1, w3, w2, layer, mod=None):
    b, rows, d = h.shape
    outs = _swiglu(h.reshape(b * rows, d), w1, w3, w2, layer, _row_tile(b * rows, 1100), 256, mod)
    act, w2b = outs[:2]
    y = _matmul([(act, 0, w2b, 0, w2b.shape[0])], _row_tile(b * rows, 550), 512)
    return y.reshape(b, rows, d), (outs[2] if mod is not None else None)


def kernel(x, c, ctx, c_ctx, w_mod, b_mod, g_mix_pre, g_mix_post, g_ffn_pre, g_ffn_post, w_ff1, w_ff3, w_ff2, e_w_in, e_g_q, e_w_uq, e_g_kv, e_w_ukv, e_s5_lam_re, e_s5_lam_im, e_s5_log_dt, e_s5_b_re, e_s5_b_im, e_s5_c_re, e_s5_c_im, e_s5_d, e_w_glu, e_w_out, o_w_in, o_lam_q1, o_lam_k1, o_lam_q2, o_lam_k2, o_g_subln, o_conv_w, o_conv_b, o_conv_norm_g, o_conv_norm_b, o_w_out):
    b, n, d = x.shape
    n_ctx = ctx.shape[1]
    t = n_ctx + n
    depth = w_mod.shape[0]
    assert depth == 2 and e_w_in.shape[0] == 1 and o_w_in.shape[0] == 1
    assert b + 1 <= SUBLANES and n % n_ctx == 0
    tr = n_ctx
    q_rank = e_g_q.shape[1]
    kv_rank = e_g_kv.shape[1]
    s5_w = S5_GROUPS * S5_GROUP

    cvec = jnp.zeros((SUBLANES, d), F32).at[:b].set(c).at[b].set(c_ctx)
    b_mod3 = b_mod.reshape(depth, 1, N_MOD * d)

    def modtab(rows):
        lat = rows[:b].reshape(b, 1, N_MOD, d)
        cx = jnp.broadcast_to(rows[b].reshape(1, 1, N_MOD, d), (b, 1, N_MOD, d))
        return jnp.concatenate([cx, lat], axis=1)

    modtabs = [modtab(_modulation(cvec, w_mod, b_mod3, 1)[0]), None]

    cos, sin = _rope_tables(n, MLA_ROPE)
    rot = _rot_matrix(MLA_ROPE)
    hp = MLA_HEAD_PAD
    cos2 = np.concatenate([cos, cos], axis=1)
    sin2 = np.concatenate([sin, sin], axis=1)
    rot2 = _embed(rot, (LANES, LANES), 0, 0, 0.0)
    rot2[DIFF_QK:, DIFF_QK:] = rot
    pair_tabs = (_embed(cos2, (t, LANES), n_ctx, 0, 1.0), _embed(sin2, (t, LANES), n_ctx, 0, 0.0),
                 jnp.asarray(rot2, BF16))
    q_tab = _embed(np.concatenate([cos, sin], axis=1), (t, hp), n_ctx, MLA_NOPE, 1.0)
    q_tab[:n_ctx, MLA_NOPE + MLA_ROPE:] = 0.0

    lw = e_w_in[0]
    cuts = [q_rank, q_rank + kv_rank, q_rank + kv_rank + MLA_ROPE]
    w_in = jnp.concatenate([lw[:, :cuts[0]], lw[:, cuts[2]:], lw[:, cuts[0]:cuts[1]], lw[:, cuts[1]:cuts[2]],
                            lw[:, cuts[1]:cuts[2]]], axis=1).astype(BF16)
    u_block = q_rank // s5_w
    kv_block = (q_rank + s5_w) // kv_rank
    kr_block = (q_rank + s5_w + kv_rank) // LANES
    w_uq = e_w_uq[0].reshape(q_rank, MLA_HEADS, MLA_NOPE + MLA_ROPE)
    w_uq = jnp.concatenate([w_uq, jnp.einsum('khj,ji->khi', w_uq[:, :, MLA_NOPE:], rot)], axis=2)
    w_uq = w_uq.reshape(q_rank, MLA_HEADS * hp).astype(BF16)
    n_attn = MLA_HEADS * MLA_V

    h = _norm_mod((ctx, x), g_mix_pre[0], modtabs[0], SH_M, SC_M, tr).reshape(b * t, d)
    tm = _row_tile(b * t, 1100)
    hin = _matmul([(h, 0, w_in, 0, d)], tm, w_in.shape[1] // 3)
    q = _matmul([(hin, 0, w_uq, 0, q_rank)], tm, 1024, gain=e_g_q[0])
    kv = _matmul([(hin, kv_block, e_w_ukv, 0, kv_rank)], tm, 1024, gain=e_g_kv[0])
    hin3 = hin.reshape(b, t, -1)
    attn = _mla_attention(q.reshape(b, t, -1), kv.reshape(b, t, -1), hin3, kr_block,
                          (q_tab,) + pair_tabs, n_ctx)
    wb, wc, a_re, a_im = _s5_weights(e_s5_lam_re[0], e_s5_lam_im[0], e_s5_log_dt[0], e_s5_b_re[0],
                                     e_s5_b_im[0], e_s5_c_re[0], e_s5_c_im[0])
    yf, yb = _s5_scan(hin3, u_block, wb, wc, a_re, a_im, tr)
    s5 = _s5_glu(yf, yb, hin3, u_block, e_s5_d[0], e_w_glu[0].astype(BF16), tr)
    y = _matmul([(attn.reshape(b * t, -1), 0, e_w_out, 0, n_attn),
                 (s5.reshape(b * t, -1), 0, e_w_out, n_attn // s5_w, s5_w)], tm, 512)
    z, h = _residual((ctx, x), y.reshape(b, t, d), g_mix_post[0], modtabs[0], GT_M, 0, 0, tr,
                     nxt=(g_ffn_pre[0], modtabs[0], SH_F, SC_F))
    y, mod1 = _ffn_matmuls(h, w_ff1, w_ff3, w_ff2, 0, mod=(cvec, w_mod, b_mod3, 1))
    modtabs[1] = modtab(mod1)
    z, h = _residual(z, y, g_ffn_post[0], modtabs[0], GT_F, 0, 0, tr,
                     nxt=(g_mix_pre[1], modtabs[1], SH_M, SC_M))

    lam_init = 0.8 - 0.6 * math.exp(-0.3 * 1)
    n_attn = DIFF_HEADS * DIFF_V
    conv_ch = o_conv_w.shape[2]

    hin3 = _matmul([(h.reshape(b * t, d), 0, o_w_in, 0, d)], tm, 512).reshape(b, t, -1)
    lam_vecs = jnp.stack([o_lam_q1[0], o_lam_k1[0], o_lam_q2[0], o_lam_k2[0]]).astype(F32)
    attn = _diff_attention(hin3, pair_tabs, lam_vecs, o_g_subln[0], lam_init, n_ctx)
    conv = _conformer_conv(hin3, 3 * n_attn // conv_ch, o_conv_w[0], o_conv_b[0], o_conv_norm_g[0],
                           o_conv_norm_b[0], n_ctx, tr)
    tml = _row_tile(b * n, 1100)
    y = _matmul([(attn.reshape(b * n, -1), 0, o_w_out, 0, n_attn),
                 (conv.reshape(b * n, -1), 0, o_w_out, n_attn // conv_ch, conv_ch)], tml, 512)
    xl, h = _residual(z, y.reshape(b, n, d), g_mix_post[1], modtabs[1], GT_M, n_ctx // tr, 1, tr,
                      nxt=(g_ffn_pre[1], modtabs[1], SH_F, SC_F))
    y, _ = _ffn_matmuls(h, w_ff1, w_ff3, w_ff2, 1)
    return _residual(xl, y, g_ffn_post[1], modtabs[1], GT_F, 0, 1, tr)
```

```python
import functools
import math

import numpy as np

import jax
import jax.numpy as jnp
from jax import lax
from jax.experimental import pallas as pl
from jax.experimental.pallas import tpu as pltpu

F32 = jnp.float32
BF16 = jnp.bfloat16

EPS = 1e-6
ROPE_BASE = 10000.0
GRID_W = 64
LOG2E = math.log2(math.e)

MLA_HEADS = 16
MLA_NOPE = 128
MLA_ROPE = 64
MLA_V = 128
MLA_HEAD_PAD = 256
MLA_HEADS_PER_STEP = 2
DIFF_HEADS = 16
DIFF_QK = 64
DIFF_V = 128
S5_GROUP = 16
S5_GROUPS = 64
S5_STATE = 64
S5_BLOCKS = 8
N_MOD = 6

LANES = 128
SUBLANES = 8
VMEM_LIMIT = 56 * 1024 * 1024

SH_M, SC_M, GT_M, SH_F, SC_F, GT_F = range(N_MOD)


def _params(*sem):
    return pltpu.CompilerParams(dimension_semantics=sem, vmem_limit_bytes=VMEM_LIMIT)


def _row_tile(rows, cap):
    k = -(-rows // cap)
    while rows % k or (rows // k) % 16:
        k += 1
    return rows // k


def _mod_rows(c_ref, w_ref, b_ref):
    c = c_ref[...]
    a = (c * jax.nn.sigmoid(c)).astype(BF16)
    return jnp.dot(a, w_ref[...].astype(BF16), preferred_element_type=F32) + b_ref[...]


def _mod_kernel(c_ref, w_ref, b_ref, o_ref):
    o_ref[...] = _mod_rows(c_ref, w_ref, b_ref)


def _modulation(cvec, w_mod, b_mod, layers):
    _, d, n = w_mod.shape
    tn = 512
    return pl.pallas_call(
        _mod_kernel,
        out_shape=jax.ShapeDtypeStruct((layers, SUBLANES, n), F32),
        grid=(layers, n // tn),
        in_specs=[pl.BlockSpec((SUBLANES, d), lambda l, j: (0, 0)),
                  pl.BlockSpec((None, d, tn), lambda l, j: (l, 0, j)),
                  pl.BlockSpec((None, 1, tn), lambda l, j: (l, 0, j))],
        out_specs=pl.BlockSpec((None, SUBLANES, tn), lambda l, j: (l, 0, j)),
        compiler_params=_params("parallel", "parallel"),
    )(cvec, w_mod, b_mod)


def _stream_specs(src, tr, d, tile_offset):
    if isinstance(src, tuple):
        return ([pl.BlockSpec((None, tr, d), lambda bi, i: (bi, 0, 0)),
                 pl.BlockSpec((None, tr, d), lambda bi, i: (bi, jnp.maximum(i - 1, 0), 0))], list(src))
    return [pl.BlockSpec((None, tr, d), lambda bi, i: (bi, i + tile_offset, 0))], [src]


def _stream_rows(x_refs):
    if len(x_refs) == 1:
        return x_refs[0][...]
    return jnp.where(pl.program_id(1) == 0, x_refs[0][...], x_refs[1][...])


def _mod_spec(d, seg_offset):
    return pl.BlockSpec((None, None, N_MOD, d), lambda bi, i: (bi, jnp.minimum(i + seg_offset, 1), 0, 0))


def _adaln(x, g_ref, m_ref, shift_row, scale_row):
    r = lax.rsqrt(jnp.mean(x * x, axis=-1, keepdims=True) + EPS)
    h = x * r * g_ref[...]
    return (h * (1.0 + m_ref[scale_row:scale_row + 1, :]) + m_ref[shift_row:shift_row + 1, :]).astype(BF16)


def _norm_mod_kernel(*refs, n_src, shift_row, scale_row):
    g_ref, m_ref, o_ref = refs[n_src:]
    o_ref[...] = _adaln(_stream_rows(refs[:n_src]), g_ref, m_ref, shift_row, scale_row)


def _norm_mod(src, gain, modtab, shift_row, scale_row, tr):
    specs, args = _stream_specs(src, tr, gain.shape[0], 0)
    b = args[0].shape[0]
    d = gain.shape[0]
    rows = sum(a.shape[1] for a in args)
    return pl.pallas_call(
        functools.partial(_norm_mod_kernel, n_src=len(args), shift_row=shift_row, scale_row=scale_row),
        out_shape=jax.ShapeDtypeStruct((b, rows, d), BF16),
        grid=(b, rows // tr),
        in_specs=specs + [pl.BlockSpec((1, d), lambda bi, i: (0, 0)), _mod_spec(d, 0)],
        out_specs=pl.BlockSpec((None, tr, d), lambda bi, i: (bi, i, 0)),
        compiler_params=_params("parallel", "parallel"),
    )(*args, gain.reshape(1, d), modtab)


def _residual_kernel(*refs, n_src, gate_row, nxt_rows):
    y_ref, g_ref, m_ref = refs[n_src:n_src + 3]
    y = y_ref[...].astype(F32)
    r = lax.rsqrt(jnp.mean(y * y, axis=-1, keepdims=True) + EPS)
    z = _stream_rows(refs[:n_src]) + m_ref[gate_row:gate_row + 1, :] * (y * r * g_ref[...])
    if nxt_rows is None:
        refs[-1][...] = z
    else:
        g2_ref, m2_ref, o_ref, h_ref = refs[n_src + 3:]
        o_ref[...] = z
        h_ref[...] = _adaln(z, g2_ref, m2_ref, *nxt_rows)


def _residual(src, y, gain, modtab, gate_row, x_tile_offset, seg_offset, tr, nxt=None):
    b, rows, d = y.shape
    specs, args = _stream_specs(src, tr, d, x_tile_offset)
    tile = lambda: pl.BlockSpec((None, tr, d), lambda bi, i: (bi, i, 0))
    vec = lambda: pl.BlockSpec((1, d), lambda bi, i: (0, 0))
    in_specs = specs + [tile(), vec(), _mod_spec(d, seg_offset)]
    args = args + [y, gain.reshape(1, d), modtab]
    out_shape = jax.ShapeDtypeStruct((b, rows, d), F32)
    out_specs = tile()
    nxt_rows = None
    if nxt is not None:
        in_specs += [vec(), _mod_spec(d, seg_offset)]
        args += [nxt[0].reshape(1, d), nxt[1]]
        nxt_rows = (nxt[2], nxt[3])
        out_shape = (out_shape, jax.ShapeDtypeStruct((b, rows, d), BF16))
        out_specs = (out_specs, tile())
    return pl.pallas_call(
        functools.partial(_residual_kernel, n_src=len(specs), gate_row=gate_row, nxt_rows=nxt_rows),
        out_shape=out_shape,
        grid=(b, rows // tr),
        in_specs=in_specs,
        out_specs=out_specs,
        compiler_params=_params("parallel", "parallel"),
    )(*args)


def _bf16(w):
    return w if w.dtype == BF16 else w.astype(BF16)


def _matmul_kernel(*refs, n_pairs, normed):
    o_ref = refs[-1]
    acc = None
    for p in range(n_pairs):
        a = refs[2 * p][...]
        if normed:
            af = a.astype(F32)
            r = lax.rsqrt(jnp.mean(af * af, axis=-1, keepdims=True) + EPS)
            a = (af * r * refs[2 * n_pairs][...]).astype(BF16)
        part = jnp.dot(a, _bf16(refs[2 * p + 1][...]), preferred_element_type=F32)
        acc = part if acc is None else acc + part
    o_ref[...] = acc.astype(o_ref.dtype)


def _weight_spec(w, k, row_block, layer, tn):
    if w.ndim == 3:
        return pl.BlockSpec((None, k, tn), lambda i, j: (layer, row_block, j))
    return pl.BlockSpec((k, tn), lambda i, j: (row_block, j))


def _matmul(pairs, tm, tn, gain=None, layer=0, out_dtype=BF16):
    rows = pairs[0][0].shape[0]
    n = pairs[0][2].shape[-1]
    tn = min(tn, n)
    in_specs, args = [], []
    for a, cb, w, rb, k in pairs:
        in_specs.append(pl.BlockSpec((tm, k), functools.partial(lambda i, j, cb: (i, cb), cb=cb)))
        in_specs.append(_weight_spec(w, k, rb, layer, tn))
        args += [a, w]
    if gain is not None:
        k = pairs[0][4]
        in_specs.append(pl.BlockSpec((1, k), lambda i, j: (0, 0)))
        args.append(gain.reshape(1, k).astype(F32))
    return pl.pallas_call(
        functools.partial(_matmul_kernel, n_pairs=len(pairs), normed=gain is not None),
        out_shape=jax.ShapeDtypeStruct((rows, n), out_dtype),
        grid=(rows // tm, n // tn),
        in_specs=in_specs,
        out_specs=pl.BlockSpec((tm, tn), lambda i, j: (i, j)),
        compiler_params=_params("parallel", "arbitrary"),
    )(*args)


def _swiglu_kernel(*refs, mod_tiles):
    a_ref, w1_ref, w3_ref, w2_ref = refs[:4]
    o_ref, w2b_ref = refs[-2 - (mod_tiles > 0):][:2]
    a = a_ref[...]
    g = jnp.dot(a, _bf16(w1_ref[...]), preferred_element_type=F32)
    u = jnp.dot(a, _bf16(w3_ref[...]), preferred_element_type=F32)
    o_ref[...] = (g * jax.nn.sigmoid(g) * u).astype(o_ref.dtype)

    @pl.when(pl.program_id(0) == 0)
    def _():
        w2b_ref[...] = w2_ref[...].astype(BF16)

    if mod_tiles:
        c_ref, wm_ref, bm_ref = refs[4:7]

        @pl.when(pl.program_id(0) * pl.num_programs(1) + pl.program_id(1) < mod_tiles)
        def _():
            refs[-1][...] = _mod_rows(c_ref, wm_ref, bm_ref)


def _swiglu(a, w1, w3, w2, layer, tm, tn, mod=None):
    rows, k = a.shape
    f = w1.shape[2]
    d = w2.shape[2]
    ni, nj = rows // tm, f // tn

    def slab(i, j):
        return jnp.where(i == 0, j, nj - 1)

    in_specs = [pl.BlockSpec((tm, k), lambda i, j: (i, 0)),
                _weight_spec(w1, k, 0, layer, tn), _weight_spec(w3, k, 0, layer, tn),
                pl.BlockSpec((None, tn, d), lambda i, j: (layer, slab(i, j), 0))]
    out_shape = [jax.ShapeDtypeStruct((rows, f), BF16), jax.ShapeDtypeStruct((f, d), BF16)]
    out_specs = [pl.BlockSpec((tm, tn), lambda i, j: (i, j)),
                 pl.BlockSpec((tn, d), lambda i, j: (slab(i, j), 0))]
    args = [a, w1, w3, w2]
    mod_tiles = 0
    if mod is not None:
        cvec, w_mod, b_mod, ml = mod
        n_mod = w_mod.shape[2]
        per_step = -(-n_mod // (LANES * ni * nj))
        while (n_mod // LANES) % per_step:
            per_step += 1
        mod_tile = per_step * LANES
        mod_tiles = n_mod // mod_tile

        def tile(i, j):
            return jnp.minimum(i * nj + j, mod_tiles - 1)

        in_specs += [pl.BlockSpec(cvec.shape, lambda i, j: (0, 0)),
                     pl.BlockSpec((None, w_mod.shape[1], mod_tile), lambda i, j: (ml, 0, tile(i, j))),
                     pl.BlockSpec((None, 1, mod_tile), lambda i, j: (ml, 0, tile(i, j)))]
        out_shape.append(jax.ShapeDtypeStruct((SUBLANES, n_mod), F32))
        out_specs.append(pl.BlockSpec((SUBLANES, mod_tile), lambda i, j: (0, tile(i, j))))
        args += [cvec, w_mod, b_mod]
    return pl.pallas_call(
        functools.partial(_swiglu_kernel, mod_tiles=mod_tiles),
        out_shape=tuple(out_shape),
        grid=(ni, nj),
        in_specs=in_specs,
        out_specs=tuple(out_specs),
        compiler_params=_params("arbitrary", "arbitrary"),
    )(*args)


def _rope_tables(n_tok, dim):
    rows = n_tok // GRID_W
    row = np.repeat(np.arange(rows, dtype=np.float32), GRID_W)
    col = np.tile(np.arange(GRID_W, dtype=np.float32), rows)
    axis_dim = dim // 2
    expo = -np.arange(0, axis_dim, 2, dtype=np.float32) / np.float32(axis_dim)
    inv = np.power(np.float32(ROPE_BASE), expo).astype(np.float32)
    ang_r = row[:, None] * inv[None, :]
    ang_c = col[:, None] * inv[None, :]
    ang = np.concatenate([ang_r, ang_r, ang_c, ang_c], axis=-1).astype(np.float32)
    return np.cos(ang), np.sin(ang)


def _rot_matrix(dim):
    q = dim // 4
    p = np.zeros((dim, dim), np.float32)
    eye = np.eye(q, dtype=np.float32)
    p[q:2 * q, 0:q] = -eye
    p[0:q, q:2 * q] = eye
    p[3 * q:, 2 * q:3 * q] = -eye
    p[2 * q:3 * q, 3 * q:] = eye
    return p


def _embed(block, shape, r0, c0, fill):
    out = np.full(shape, fill, np.float32)
    out[r0:r0 + block.shape[0], c0:c0 + block.shape[1]] = block
    return out


def _rope(x_bf16, cos, sin, rot):
    turned = jnp.dot(x_bf16, rot, preferred_element_type=F32)
    return x_bf16.astype(F32) * cos + turned * sin


KEY_CHUNK = 1024


def _softmax_values(scores, vones):
    keys = scores[0].shape[1]
    maxes = [jnp.max(s, axis=-1, keepdims=True) for s in scores]
    accs = [None] * len(scores)
    for c0 in range(0, keys, KEY_CHUNK):
        c1 = min(c0 + KEY_CHUNK, keys)
        for i, (s, m, vone) in enumerate(zip(scores, maxes, vones)):
            e = jnp.exp2(s[:, c0:c1] - m).astype(BF16)
            part = jnp.dot(e, vone[c0:c1, :], preferred_element_type=F32)
            accs[i] = part if accs[i] is None else accs[i] + part
    dv = accs[0].shape[1] // 2
    return [a[:, :dv] / a[:, dv:] for a in accs]


def _const_spec(shape):
    return pl.BlockSpec(shape, lambda bi, h, i: (0,) * len(shape), pipeline_mode=pl.Buffered(1))


_NT = (((1,), (1,)), ((), ()))


def _mla_kernel(q_ref, kv_ref, kr_ref, tq_ref, ck_ref, sk_ref, pk_ref, o_ref, kcat_ref, vone_ref,
                *, n_ctx, scale, heads):
    qi = pl.program_id(2)
    hp = MLA_HEAD_PAD

    @pl.when(qi == 0)
    def _():
        kr = _rope(kr_ref[...], ck_ref[...], sk_ref[...], pk_ref[...]).astype(BF16)
        ones = jnp.ones((kr.shape[0], hp - MLA_V), BF16)
        for h in range(heads):
            kcat_ref[h, :, :MLA_NOPE] = kv_ref[:, h * hp:h * hp + MLA_NOPE]
            kcat_ref[h, :, MLA_NOPE:] = kr
            vone_ref[h, :, :MLA_V] = kv_ref[:, h * hp + MLA_NOPE:(h + 1) * hp]
            vone_ref[h, :, MLA_V:] = ones

    def attend(rows):
        scores = []
        for h in range(heads):
            q = q_ref[:, h * hp:(h + 1) * hp].astype(F32) * tq_ref[...] * scale
            scores.append(lax.dot_general(q.astype(BF16), kcat_ref[h, :rows, :], _NT,
                                          preferred_element_type=F32))
        outs = _softmax_values(scores, [vone_ref.at[h, :rows, :] for h in range(heads)])
        for h in range(heads):
            o_ref[:, h * MLA_V:(h + 1) * MLA_V] = outs[h].astype(o_ref.dtype)

    @pl.when(qi == 0)
    def _():
        attend(n_ctx)

    @pl.when(qi > 0)
    def _():
        attend(kcat_ref.shape[1])


def _mla_attention(q, kv, hin, kr_block, tabs, n_ctx):
    b, t, _ = q.shape
    tq = n_ctx
    tq_tab, ck, sk, pk = tabs
    scale = (MLA_NOPE + MLA_ROPE) ** -0.5 * LOG2E
    hp = MLA_HEAD_PAD
    hs = MLA_HEADS_PER_STEP
    assert hp == 2 * MLA_V and hp == MLA_NOPE + 2 * MLA_ROPE == MLA_NOPE + LANES
    return pl.pallas_call(
        functools.partial(_mla_kernel, n_ctx=n_ctx, scale=scale, heads=hs),
        out_shape=jax.ShapeDtypeStruct((b, t, MLA_HEADS * MLA_V), BF16),
        grid=(b, MLA_HEADS // hs, t // tq),
        in_specs=[pl.BlockSpec((None, tq, hs * hp), lambda bi, h, i: (bi, i, h)),
                  pl.BlockSpec((None, t, hs * hp), lambda bi, h, i: (bi, 0, h)),
                  pl.BlockSpec((None, t, LANES), lambda bi, h, i: (bi, 0, kr_block)),
                  pl.BlockSpec((tq, hp), lambda bi, h, i: (i, 0)),
                  _const_spec((t, LANES)), _const_spec((t, LANES)), _const_spec((LANES, LANES))],
        out_specs=pl.BlockSpec((None, tq, hs * MLA_V), lambda bi, h, i: (bi, i, h)),
        scratch_shapes=[pltpu.VMEM((hs, t, hp), BF16), pltpu.VMEM((hs, t, hp), BF16)],
        compiler_params=_params("parallel", "parallel", "arbitrary"),
    )(q, kv, hin, tq_tab, ck, sk, pk)


DIFF_TILES_PER_STEP = 2


def _diff_kernel(*refs, tiles, scale, lam_init):
    q_refs = refs[:tiles]
    k_ref, v_ref, cq_ref, sq_ref, ck_ref, sk_ref, p_ref, lam_ref, g_ref, o_ref, kro_ref, vone_ref = refs[tiles:]
    tq = q_refs[0].shape[0]

    @pl.when(pl.program_id(2) == 0)
    def _():
        kro_ref[...] = _rope(k_ref[...], ck_ref[...], sk_ref[...], p_ref[...]).astype(BF16)
        vone_ref[:, :DIFF_V] = v_ref[...]
        vone_ref[:, DIFF_V:] = jnp.ones((v_ref.shape[0], DIFF_V), BF16)

    lv = lam_ref[...]
    lam = (jnp.exp(jnp.sum(lv[0:1] * lv[1:2], axis=-1, keepdims=True))
           - jnp.exp(jnp.sum(lv[2:3] * lv[3:4], axis=-1, keepdims=True)) + lam_init)

    def scores(qm):
        return lax.dot_general(qm.astype(BF16), kro_ref[...], _NT, preferred_element_type=F32)

    chains = []
    for r, q_ref in enumerate(q_refs):
        rows = slice(r * tq, (r + 1) * tq)
        q = _rope(q_ref[...], cq_ref[rows, :], sq_ref[rows, :], p_ref[...]) * scale
        lane = lax.broadcasted_iota(jnp.int32, q.shape, 1)
        chains += [scores(jnp.where(lane < DIFF_QK, q, 0.0)), scores(jnp.where(lane >= DIFF_QK, q, 0.0))]
    outs = _softmax_values(chains, [vone_ref] * len(chains))
    for r in range(tiles):
        o = outs[2 * r] - lam * outs[2 * r + 1]
        norm = lax.rsqrt(jnp.mean(o * o, axis=-1, keepdims=True) + EPS)
        o_ref[r * tq:(r + 1) * tq, :] = (o * norm * g_ref[...] * (1.0 - lam_init)).astype(o_ref.dtype)


def _diff_attention(hin, tabs, lam_vecs, g_subln, lam_init, n_ctx):
    b, t, _ = hin.shape
    tq = n_ctx
    n = t - n_ctx
    cos, sin, p = tabs
    h_all = DIFF_HEADS
    tiles = DIFF_TILES_PER_STEP
    off = n_ctx // tq
    q_specs = [pl.BlockSpec((None, tq, LANES), functools.partial(lambda bi, h, i, r: (bi, tiles * i + off + r, h), r=r))
               for r in range(tiles)]
    q_tab = pl.BlockSpec((tiles * tq, LANES), lambda bi, h, i: (i, 0))
    return pl.pallas_call(
        functools.partial(_diff_kernel, tiles=tiles, scale=DIFF_QK ** -0.5 * LOG2E, lam_init=lam_init),
        out_shape=jax.ShapeDtypeStruct((b, n, h_all * DIFF_V), BF16),
        grid=(b, h_all, n // (tiles * tq)),
        in_specs=q_specs + [
            pl.BlockSpec((None, t, LANES), lambda bi, h, i: (bi, 0, h_all + h)),
            pl.BlockSpec((None, t, LANES), lambda bi, h, i: (bi, 0, 2 * h_all + h)),
            q_tab, q_tab,
            _const_spec((t, LANES)), _const_spec((t, LANES)),
            _const_spec((LANES, LANES)), _const_spec((4, DIFF_QK)), _const_spec((1, DIFF_V))],
        out_specs=pl.BlockSpec((None, tiles * tq, DIFF_V), lambda bi, h, i: (bi, i, h)),
        scratch_shapes=[pltpu.VMEM((t, LANES), BF16), pltpu.VMEM((t, 2 * DIFF_V), BF16)],
        compiler_params=_params("parallel", "parallel", "arbitrary"),
    )(*([hin] * tiles), hin, hin, cos[n_ctx:], sin[n_ctx:], cos, sin, p, lam_vecs,
      g_subln.reshape(1, DIFF_V).astype(F32))


def _s5_kernel(uf_ref, ub_ref, wb_ref, wc_ref, are_ref, aim_ref, yf_ref, yb_ref, s_ref, hs_ref, h_ref,
               *, tc):
    half = S5_GROUPS // S5_BLOCKS * S5_STATE
    gw = S5_GROUPS // S5_BLOCKS * S5_GROUP
    nl = half // LANES
    u_refs = (uf_ref, ub_ref)
    y_refs = (yf_ref, yb_ref)

    @pl.when(pl.program_id(1) == 0)
    def _():
        h_ref[...] = jnp.zeros_like(h_ref)

    for d in range(2):
        for j in range(S5_BLOCKS):
            bu = jnp.dot(u_refs[d][:, j * gw:(j + 1) * gw], wb_ref[d, j], preferred_element_type=F32)
            for c in range(2 * nl):
                s_ref.at[d, c][pl.ds(j, tc, stride=S5_BLOCKS), :] = bu[:, c * LANES:(c + 1) * LANES]

    a_re = [[are_ref[d, :, c * LANES:(c + 1) * LANES] for c in range(nl)] for d in range(2)]
    a_im = [[aim_ref[d, :, c * LANES:(c + 1) * LANES] for c in range(nl)] for d in range(2)]

    def step(t, h):
        rows = (t * S5_BLOCKS, (tc - 1 - t) * S5_BLOCKS)
        out = []
        for d in range(2):
            row = pl.multiple_of(rows[d], S5_BLOCKS)
            new = [None] * (2 * nl)
            for c in range(nl):
                h_re, h_im = h[d * 2 * nl + c], h[d * 2 * nl + nl + c]
                new[c] = a_re[d][c] * h_re - a_im[d][c] * h_im + s_ref[d, c, pl.ds(row, S5_BLOCKS), :]
                new[nl + c] = (a_re[d][c] * h_im + a_im[d][c] * h_re
                               + s_ref[d, nl + c, pl.ds(row, S5_BLOCKS), :])
            for c in range(2 * nl):
                hs_ref[d, c, pl.ds(row, S5_BLOCKS), :] = new[c]
            out += new
        return tuple(out)

    init = tuple(h_ref[d, c] for d in range(2) for c in range(2 * nl))
    h = lax.fori_loop(0, tc, step, init, unroll=4)
    for d in range(2):
        for c in range(2 * nl):
            h_ref[d, c] = h[d * 2 * nl + c]

    for d in range(2):
        for j in range(S5_BLOCKS):
            hj = jnp.concatenate([hs_ref.at[d, c][pl.ds(j, tc, stride=S5_BLOCKS), :] for c in range(2 * nl)],
                                 axis=1).astype(BF16)
            y_refs[d][:, j * gw:(j + 1) * gw] = jnp.dot(hj, wc_ref[d, j], preferred_element_type=F32)


def _s5_scan(hin, u_block, wb, wc, a_re, a_im, tc):
    b, t, _ = hin.shape
    width = S5_GROUPS * S5_GROUP
    half = S5_GROUPS // S5_BLOCKS * S5_STATE
    nchunk = t // tc

    def back(k):
        return jnp.where(k == 0, 0, nchunk - k)

    full = lambda a: pl.BlockSpec(a.shape, lambda bi, k: (0,) * a.ndim)
    out = jax.ShapeDtypeStruct((b, t, width), F32)
    return pl.pallas_call(
        functools.partial(_s5_kernel, tc=tc),
        out_shape=(out, out),
        grid=(b, nchunk),
        in_specs=[pl.BlockSpec((None, tc, width), lambda bi, k: (bi, k, u_block)),
                  pl.BlockSpec((None, tc, width), lambda bi, k: (bi, back(k), u_block)),
                  full(wb), full(wc), full(a_re), full(a_im)],
        out_specs=(pl.BlockSpec((None, tc, width), lambda bi, k: (bi, k, 0)),
                   pl.BlockSpec((None, tc, width), lambda bi, k: (bi, back(k), 0))),
        scratch_shapes=[pltpu.VMEM((2, 2 * half // LANES, tc * S5_BLOCKS, LANES), F32),
                        pltpu.VMEM((2, 2 * half // LANES, tc * S5_BLOCKS, LANES), F32),
                        pltpu.VMEM((2, 2 * half // LANES, S5_BLOCKS, LANES), F32)],
        compiler_params=_params("parallel", "arbitrary"),
    )(hin, hin, wb, wc, a_re, a_im)


def _s5_weights(lam_re, lam_im, log_dt, b_re, b_im, c_re, c_im):
    lr = lam_re.astype(F32)
    li = lam_im.astype(F32)
    dt = jnp.exp(log_dt.astype(F32))[..., None]
    mag = jnp.exp(lr * dt)
    ab_re = mag * jnp.cos(li * dt)
    ab_im = mag * jnp.sin(li * dt)
    den = lr * lr + li * li
    num_re = ab_re - 1.0
    coef_re = ((num_re * lr + ab_im * li) / den)[..., None]
    coef_im = ((ab_im * lr - num_re * li) / den)[..., None]
    br = b_re.astype(F32)
    bi = b_im.astype(F32)
    bb_re = coef_re * br - coef_im * bi
    bb_im = coef_re * bi + coef_im * br
    nb = S5_BLOCKS
    gb = S5_GROUPS // nb
    eye = jnp.eye(gb, dtype=F32)

    def drive(bb):
        w = jnp.einsum('djgpi,gh->djgihp', bb.reshape(2, nb, gb, S5_STATE, S5_GROUP), eye)
        return w.reshape(2, nb, gb * S5_GROUP, gb * S5_STATE)

    def read(c):
        w = jnp.einsum('djgip,gh->djgphi', c.astype(F32).reshape(2, nb, gb, S5_GROUP, S5_STATE), eye)
        return w.reshape(2, nb, gb * S5_STATE, gb * S5_GROUP)

    wb = jnp.concatenate([drive(bb_re), drive(bb_im)], axis=-1).astype(BF16)
    wc = jnp.concatenate([read(c_re), -read(c_im)], axis=-2).astype(BF16)
    return wb, wc, ab_re.reshape(2, nb, gb * S5_STATE), ab_im.reshape(2, nb, gb * S5_STATE)


def _s5_glu_kernel(yf_ref, yb_ref, u_ref, d_ref, w_ref, o_ref):
    y = yf_ref[...] + yb_ref[...] + d_ref[...] * u_ref[...].astype(F32)
    r = jnp.dot(jax.nn.gelu(y).astype(BF16), w_ref[...], preferred_element_type=F32)
    n = r.shape[1] // 2
    o_ref[...] = (r[:, :n] * jax.nn.sigmoid(r[:, n:])).astype(o_ref.dtype)


def _s5_glu(yf, yb, hin, u_block, d_skip, w_glu, tm):
    b, t, width = yf.shape
    return pl.pallas_call(
        _s5_glu_kernel,
        out_shape=jax.ShapeDtypeStruct((b, t, width), BF16),
        grid=(b, t // tm),
        in_specs=[pl.BlockSpec((None, tm, width), lambda bi, i: (bi, i, 0)),
                  pl.BlockSpec((None, tm, width), lambda bi, i: (bi, i, 0)),
                  pl.BlockSpec((None, tm, width), lambda bi, i: (bi, i, u_block)),
                  pl.BlockSpec((1, width), lambda bi, i: (0, 0)),
                  pl.BlockSpec(w_glu.shape, lambda bi, i: (0, 0))],
        out_specs=pl.BlockSpec((None, tm, width), lambda bi, i: (bi, i, 0)),
        compiler_params=_params("parallel", "parallel"),
    )(yf, yb, hin, d_skip.reshape(1, width).astype(F32), w_glu)


CONV_HALO = 16


def _conv_kernel(ap_ref, gp_ref, ac_ref, gc_ref, an_ref, gn_ref, w_ref, b_ref, lg_ref, lb_ref,
                 o_ref, v_ref, sh_ref, y_ref, *, tr, taps):
    i = pl.program_id(1)
    last = pl.num_programs(1) - 1

    def glu(a_ref, g_ref):
        return a_ref[...].astype(F32) * jax.nn.sigmoid(g_ref[...].astype(F32))

    v_ref[0:CONV_HALO, :] = jnp.where(i > 0, glu(ap_ref, gp_ref), 0.0)
    v_ref[CONV_HALO:CONV_HALO + tr, :] = glu(ac_ref, gc_ref)
    v_ref[CONV_HALO + tr:, :] = jnp.where(i < last, glu(an_ref, gn_ref), 0.0)

    span = v_ref.shape[0] - SUBLANES
    for r in range(1, SUBLANES):
        sh_ref[r - 1, 0:span, :] = v_ref[r:r + span, :]

    first = CONV_HALO - taps // 2
    for c0 in range(0, v_ref.shape[1], LANES):
        acc = jnp.zeros((tr, LANES), F32)
        for k in range(taps):
            tile, r = divmod(first + k, SUBLANES)
            src = v_ref if r == 0 else sh_ref.at[r - 1]
            window = src[tile * SUBLANES:tile * SUBLANES + tr, c0:c0 + LANES]
            acc = acc + w_ref[k:k + 1, c0:c0 + LANES] * window
        y_ref[:, c0:c0 + LANES] = acc

    y = y_ref[...] + b_ref[...]
    yc = y - jnp.mean(y, axis=-1, keepdims=True)
    yn = yc * lax.rsqrt(jnp.mean(yc * yc, axis=-1, keepdims=True) + EPS) * lg_ref[...] + lb_ref[...]
    o_ref[...] = (yn * jax.nn.sigmoid(yn)).astype(o_ref.dtype)


def _conformer_conv(hin, a_block, w_dw, b_dw, g_n, b_n, n_ctx, tr):
    b, t, _ = hin.shape
    n = t - n_ctx
    taps, ch = w_dw.shape
    hb = tr // CONV_HALO
    off = n_ctx // tr
    last_halo = t // CONV_HALO - 1

    def prev_map(cb):
        return lambda bi, i: (bi, (i + off) * hb - 1, cb)

    def next_map(cb):
        return lambda bi, i: (bi, jnp.minimum((i + off + 1) * hb, last_halo), cb)

    def cur_map(cb):
        return lambda bi, i: (bi, i + off, cb)

    vec = lambda v: v.reshape(1, ch).astype(F32)
    halo = lambda m: pl.BlockSpec((None, CONV_HALO, ch), m)
    full = lambda shape: pl.BlockSpec(shape, lambda bi, i: (0, 0))
    return pl.pallas_call(
        functools.partial(_conv_kernel, tr=tr, taps=taps),
        out_shape=jax.ShapeDtypeStruct((b, n, ch), BF16),
        grid=(b, n // tr),
        in_specs=[halo(prev_map(a_block)), halo(prev_map(a_block + 1)),
                  pl.BlockSpec((None, tr, ch), cur_map(a_block)),
                  pl.BlockSpec((None, tr, ch), cur_map(a_block + 1)),
                  halo(next_map(a_block)), halo(next_map(a_block + 1)),
                  full((taps, ch)), full((1, ch)), full((1, ch)), full((1, ch))],
        out_specs=pl.BlockSpec((None, tr, ch), lambda bi, i: (bi, i, 0)),
        scratch_shapes=[pltpu.VMEM((tr + 2 * CONV_HALO, ch), F32),
                        pltpu.VMEM((SUBLANES - 1, tr + 2 * CONV_HALO, ch), F32),
                        pltpu.VMEM((tr, ch), F32)],
        compiler_params=_params("parallel", "parallel"),
    )(hin, hin, hin, hin, hin, hin, w_dw.astype(F32), vec(b_dw), vec(g_n), vec(b_n))


def _ffn_matmuls(h, w1, w3, w2, layer, mod=None):
    b, rows, d = h.shape
    outs = _swiglu(h.reshape(b * rows, d), w1, w3, w2, layer, _row_tile(b * rows, 1100), 256, mod)
    act, w2b = outs[:2]
    y = _matmul([(act, 0, w2b, 0, w2b.shape[0])], _row_tile(b * rows, 550), 512)
    return y.reshape(b, rows, d), (outs[2] if mod is not None else None)


def kernel(x, c, ctx, c_ctx, w_mod, b_mod, g_mix_pre, g_mix_post, g_ffn_pre, g_ffn_post, w_ff1, w_ff3, w_ff2, e_w_in, e_g_q, e_w_uq, e_g_kv, e_w_ukv, e_s5_lam_re, e_s5_lam_im, e_s5_log_dt, e_s5_b_re, e_s5_b_im, e_s5_c_re, e_s5_c_im, e_s5_d, e_w_glu, e_w_out, o_w_in, o_lam_q1, o_lam_k1, o_lam_q2, o_lam_k2, o_g_subln, o_conv_w, o_conv_b, o_conv_norm_g, o_conv_norm_b, o_w_out):
    b, n, d = x.shape
    n_ctx = ctx.shape[1]
    t = n_ctx + n
    depth = w_mod.shape[0]
    assert depth == 2 and e_w_in.shape[0] == 1 and o_w_in.shape[0] == 1
    assert b + 1 <= SUBLANES and n % n_ctx == 0
    tr = n_ctx
    q_rank = e_g_q.shape[1]
    kv_rank = e_g_kv.shape[1]
    s5_w = S5_GROUPS * S5_GROUP

    cvec = jnp.zeros((SUBLANES, d), F32).at[:b].set(c).at[b].set(c_ctx)
    b_mod3 = b_mod.reshape(depth, 1, N_MOD * d)

    def modtab(rows):
        lat = rows[:b].reshape(b, 1, N_MOD, d)
        cx = jnp.broadcast_to(rows[b].reshape(1, 1, N_MOD, d), (b, 1, N_MOD, d))
        return jnp.concatenate([cx, lat], axis=1)

    modtabs = [modtab(_modulation(cvec, w_mod, b_mod3, 1)[0]), None]

    cos, sin = _rope_tables(n, MLA_ROPE)
    rot = _rot_matrix(MLA_ROPE)
    hp = MLA_HEAD_PAD
    cos2 = np.concatenate([cos, cos], axis=1)
    sin2 = np.concatenate([sin, sin], axis=1)
    rot2 = _embed(rot, (LANES, LANES), 0, 0, 0.0)
    rot2[DIFF_QK:, DIFF_QK:] = rot
    pair_tabs = (_embed(cos2, (t, LANES), n_ctx, 0, 1.0), _embed(sin2, (t, LANES), n_ctx, 0, 0.0),
                 jnp.asarray(rot2, BF16))
    q_tab = _embed(np.concatenate([cos, sin], axis=1), (t, hp), n_ctx, MLA_NOPE, 1.0)
    q_tab[:n_ctx, MLA_NOPE + MLA_ROPE:] = 0.0

    lw = e_w_in[0]
    cuts = [q_rank, q_rank + kv_rank, q_rank + kv_rank + MLA_ROPE]
    w_in = jnp.concatenate([lw[:, :cuts[0]], lw[:, cuts[2]:], lw[:, cuts[0]:cuts[1]], lw[:, cuts[1]:cuts[2]],
                            lw[:, cuts[1]:cuts[2]]], axis=1).astype(BF16)
    u_block = q_rank // s5_w
    kv_block = (q_rank + s5_w) // kv_rank
    kr_block = (q_rank + s5_w + kv_rank) // LANES
    w_uq = e_w_uq[0].reshape(q_rank, MLA_HEADS, MLA_NOPE + MLA_ROPE)
    w_uq = jnp.concatenate([w_uq, jnp.einsum('khj,ji->khi', w_uq[:, :, MLA_NOPE:], rot)], axis=2)
    w_uq = w_uq.reshape(q_rank, MLA_HEADS * hp).astype(BF16)
    n_attn = MLA_HEADS * MLA_V

    h = _norm_mod((ctx, x), g_mix_pre[0], modtabs[0], SH_M, SC_M, tr).reshape(b * t, d)
    tm = _row_tile(b * t, 1100)
    hin = _matmul([(h, 0, w_in, 0, d)], tm, w_in.shape[1] // 3)
    q = _matmul([(hin, 0, w_uq, 0, q_rank)], tm, 2048, gain=e_g_q[0])
    kv = _matmul([(hin, kv_block, e_w_ukv, 0, kv_rank)], tm, 2048, gain=e_g_kv[0])
    hin3 = hin.reshape(b, t, -1)
    attn = _mla_attention(q.reshape(b, t, -1), kv.reshape(b, t, -1), hin3, kr_block,
                          (q_tab,) + pair_tabs, n_ctx)
    wb, wc, a_re, a_im = _s5_weights(e_s5_lam_re[0], e_s5_lam_im[0], e_s5_log_dt[0], e_s5_b_re[0],
                                     e_s5_b_im[0], e_s5_c_re[0], e_s5_c_im[0])
    yf, yb = _s5_scan(hin3, u_block, wb, wc, a_re, a_im, tr)
    s5 = _s5_glu(yf, yb, hin3, u_block, e_s5_d[0], e_w_glu[0].astype(BF16), tr)
    y = _matmul([(attn.reshape(b * t, -1), 0, e_w_out, 0, n_attn),
                 (s5.reshape(b * t, -1), 0, e_w_out, n_attn // s5_w, s5_w)], tm, 512)
    z, h = _residual((ctx, x), y.reshape(b, t, d), g_mix_post[0], modtabs[0], GT_M, 0, 0, tr,
                     nxt=(g_ffn_pre[0], modtabs[0], SH_F, SC_F))
    y, mod1 = _ffn_matmuls(h, w_ff1, w_ff3, w_ff2, 0, mod=(cvec, w_mod, b_mod3, 1))
    modtabs[1] = modtab(mod1)
    z, h = _residual(z, y, g_ffn_post[0], modtabs[0], GT_F, 0, 0, tr,
                     nxt=(g_mix_pre[1], modtabs[1], SH_M, SC_M))

    lam_init = 0.8 - 0.6 * math.exp(-0.3 * 1)
    n_attn = DIFF_HEADS * DIFF_V
    conv_ch = o_conv_w.shape[2]

    hin3 = _matmul([(h.reshape(b * t, d), 0, o_w_in, 0, d)], tm, 512).reshape(b, t, -1)
    lam_vecs = jnp.stack([o_lam_q1[0], o_lam_k1[0], o_lam_q2[0], o_lam_k2[0]]).astype(F32)
    attn = _diff_attention(hin3, pair_tabs, lam_vecs, o_g_subln[0], lam_init, n_ctx)
    conv = _conformer_conv(hin3, 3 * n_attn // conv_ch, o_conv_w[0], o_conv_b[0], o_conv_norm_g[0],
                           o_conv_norm_b[0], n_ctx, tr)
    tml = _row_tile(b * n, 1100)
    y = _matmul([(attn.reshape(b * n, -1), 0, o_w_out, 0, n_attn),
                 (conv.reshape(b * n, -1), 0, o_w_out, n_attn // conv_ch, conv_ch)], tml, 512)
    xl, h = _residual(z, y.reshape(b, n, d), g_mix_post[1], modtabs[1], GT_M, n_ctx // tr, 1, tr,
                      nxt=(g_ffn_pre[1], modtabs[1], SH_F, SC_F))
    y, _ = _ffn_matmuls(h, w_ff1, w_ff3, w_ff2, 1)
    return _residual(xl, y, g_ffn_post[1], modtabs[1], GT_F, 0, 1, tr)
```

```python
import functools
import math

import numpy as np

import jax
import jax.numpy as jnp
from jax import lax
from jax.experimental import pallas as pl
from jax.experimental.pallas import tpu as pltpu

F32 = jnp.float32
BF16 = jnp.bfloat16

EPS = 1e-6
ROPE_BASE = 10000.0
GRID_W = 64
LOG2E = math.log2(math.e)

MLA_HEADS = 16
MLA_NOPE = 128
MLA_ROPE = 64
MLA_V = 128
MLA_HEAD_PAD = 256
MLA_HEADS_PER_STEP = 2
DIFF_HEADS = 16
DIFF_QK = 64
DIFF_V = 128
S5_GROUP = 16
S5_GROUPS = 64
S5_STATE = 64
S5_BLOCKS = 8
N_MOD = 6

LANES = 128
SUBLANES = 8
VMEM_LIMIT = 56 * 1024 * 1024

SH_M, SC_M, GT_M, SH_F, SC_F, GT_F = range(N_MOD)


def _params(*sem):
    return pltpu.CompilerParams(dimension_semantics=sem, vmem_limit_bytes=VMEM_LIMIT)


def _row_tile(rows, cap):
    k = -(-rows // cap)
    while rows % k or (rows // k) % 16:
        k += 1
    return rows // k


def _mod_rows(c_ref, w_ref, b_ref):
    c = c_ref[...]
    a = (c * jax.nn.sigmoid(c)).astype(BF16)
    return jnp.dot(a, w_ref[...].astype(BF16), preferred_element_type=F32) + b_ref[...]


def _mod_kernel(c_ref, w_ref, b_ref, o_ref):
    o_ref[...] = _mod_rows(c_ref, w_ref, b_ref)


def _modulation(cvec, w_mod, b_mod, layers):
    _, d, n = w_mod.shape
    tn = 512
    return pl.pallas_call(
        _mod_kernel,
        out_shape=jax.ShapeDtypeStruct((layers, SUBLANES, n), F32),
        grid=(layers, n // tn),
        in_specs=[pl.BlockSpec((SUBLANES, d), lambda l, j: (0, 0)),
                  pl.BlockSpec((None, d, tn), lambda l, j: (l, 0, j)),
                  pl.BlockSpec((None, 1, tn), lambda l, j: (l, 0, j))],
        out_specs=pl.BlockSpec((None, SUBLANES, tn), lambda l, j: (l, 0, j)),
        compiler_params=_params("parallel", "parallel"),
    )(cvec, w_mod, b_mod)


def _stream_specs(src, tr, d, tile_offset):
    if isinstance(src, tuple):
        return ([pl.BlockSpec((None, tr, d), lambda bi, i: (bi, 0, 0)),
                 pl.BlockSpec((None, tr, d), lambda bi, i: (bi, jnp.maximum(i - 1, 0), 0))], list(src))
    return [pl.BlockSpec((None, tr, d), lambda bi, i: (bi, i + tile_offset, 0))], [src]


def _stream_rows(x_refs):
    if len(x_refs) == 1:
        return x_refs[0][...]
    return jnp.where(pl.program_id(1) == 0, x_refs[0][...], x_refs[1][...])


def _mod_spec(d, seg_offset):
    return pl.BlockSpec((None, None, N_MOD, d), lambda bi, i: (bi, jnp.minimum(i + seg_offset, 1), 0, 0))


def _adaln(x, g_ref, m_ref, shift_row, scale_row):
    r = lax.rsqrt(jnp.mean(x * x, axis=-1, keepdims=True) + EPS)
    h = x * r * g_ref[...]
    return (h * (1.0 + m_ref[scale_row:scale_row + 1, :]) + m_ref[shift_row:shift_row + 1, :]).astype(BF16)


def _norm_mod_kernel(*refs, n_src, shift_row, scale_row):
    g_ref, m_ref, o_ref = refs[n_src:]
    o_ref[...] = _adaln(_stream_rows(refs[:n_src]), g_ref, m_ref, shift_row, scale_row)


def _norm_mod(src, gain, modtab, shift_row, scale_row, tr):
    specs, args = _stream_specs(src, tr, gain.shape[0], 0)
    b = args[0].shape[0]
    d = gain.shape[0]
    rows = sum(a.shape[1] for a in args)
    return pl.pallas_call(
        functools.partial(_norm_mod_kernel, n_src=len(args), shift_row=shift_row, scale_row=scale_row),
        out_shape=jax.ShapeDtypeStruct((b, rows, d), BF16),
        grid=(b, rows // tr),
        in_specs=specs + [pl.BlockSpec((1, d), lambda bi, i: (0, 0)), _mod_spec(d, 0)],
        out_specs=pl.BlockSpec((None, tr, d), lambda bi, i: (bi, i, 0)),
        compiler_params=_params("parallel", "parallel"),
    )(*args, gain.reshape(1, d), modtab)


def _residual_kernel(*refs, n_src, gate_row, nxt_rows):
    y_ref, g_ref, m_ref = refs[n_src:n_src + 3]
    y = y_ref[...].astype(F32)
    r = lax.rsqrt(jnp.mean(y * y, axis=-1, keepdims=True) + EPS)
    z = _stream_rows(refs[:n_src]) + m_ref[gate_row:gate_row + 1, :] * (y * r * g_ref[...])
    if nxt_rows is None:
        refs[-1][...] = z
    else:
        g2_ref, m2_ref, o_ref, h_ref = refs[n_src + 3:]
        o_ref[...] = z
        h_ref[...] = _adaln(z, g2_ref, m2_ref, *nxt_rows)


def _residual(src, y, gain, modtab, gate_row, x_tile_offset, seg_offset, tr, nxt=None, y_rotated=False):
    b, rows, d = y.shape
    specs, args = _stream_specs(src, tr, d, x_tile_offset)
    tile = lambda: pl.BlockSpec((None, tr, d), lambda bi, i: (bi, i, 0))
    vec = lambda: pl.BlockSpec((1, d), lambda bi, i: (0, 0))
    y_spec = tile()
    if y_rotated:
        y_spec = pl.BlockSpec((None, tr, d), lambda bi, i: (bi, _rotated_tile(i, rows // tr), 0))
    in_specs = specs + [y_spec, vec(), _mod_spec(d, seg_offset)]
    args = args + [y, gain.reshape(1, d), modtab]
    out_shape = jax.ShapeDtypeStruct((b, rows, d), F32)
    out_specs = tile()
    nxt_rows = None
    if nxt is not None:
        in_specs += [vec(), _mod_spec(d, seg_offset)]
        args += [nxt[0].reshape(1, d), nxt[1]]
        nxt_rows = (nxt[2], nxt[3])
        out_shape = (out_shape, jax.ShapeDtypeStruct((b, rows, d), BF16))
        out_specs = (out_specs, tile())
    return pl.pallas_call(
        functools.partial(_residual_kernel, n_src=len(specs), gate_row=gate_row, nxt_rows=nxt_rows),
        out_shape=out_shape,
        grid=(b, rows // tr),
        in_specs=in_specs,
        out_specs=out_specs,
        compiler_params=_params("parallel", "parallel"),
    )(*args)


def _bf16(w):
    return w if w.dtype == BF16 else w.astype(BF16)


def _matmul_kernel(*refs, n_pairs, normed):
    o_ref = refs[-1]
    acc = None
    for p in range(n_pairs):
        a = refs[2 * p][...]
        if normed:
            af = a.astype(F32)
            r = lax.rsqrt(jnp.mean(af * af, axis=-1, keepdims=True) + EPS)
            a = (af * r * refs[2 * n_pairs][...]).astype(BF16)
        part = jnp.dot(a, _bf16(refs[2 * p + 1][...]), preferred_element_type=F32)
        acc = part if acc is None else acc + part
    o_ref[...] = acc.astype(o_ref.dtype)


def _weight_spec(w, k, row_block, layer, tn):
    if w.ndim == 3:
        return pl.BlockSpec((None, k, tn), lambda i, j: (layer, row_block, j))
    return pl.BlockSpec((k, tn), lambda i, j: (row_block, j))


def _matmul(pairs, tm, tn, gain=None, layer=0, out_dtype=BF16):
    rows = pairs[0][0].shape[0]
    n = pairs[0][2].shape[-1]
    tn = min(tn, n)
    in_specs, args = [], []
    for a, cb, w, rb, k in pairs:
        in_specs.append(pl.BlockSpec((tm, k), functools.partial(lambda i, j, cb: (i, cb), cb=cb)))
        in_specs.append(_weight_spec(w, k, rb, layer, tn))
        args += [a, w]
    if gain is not None:
        k = pairs[0][4]
        in_specs.append(pl.BlockSpec((1, k), lambda i, j: (0, 0)))
        args.append(gain.reshape(1, k).astype(F32))
    return pl.pallas_call(
        functools.partial(_matmul_kernel, n_pairs=len(pairs), normed=gain is not None),
        out_shape=jax.ShapeDtypeStruct((rows, n), out_dtype),
        grid=(rows // tm, n // tn),
        in_specs=in_specs,
        out_specs=pl.BlockSpec((tm, tn), lambda i, j: (i, j)),
        compiler_params=_params("parallel", "arbitrary"),
    )(*args)


def _swiglu_kernel(*refs, mod_tiles):
    a_ref, w1_ref, w3_ref, w2_ref = refs[:4]
    o_ref, w2b_ref = refs[-2 - (mod_tiles > 0):][:2]
    a = a_ref[...]
    g = jnp.dot(a, _bf16(w1_ref[...]), preferred_element_type=F32)
    u = jnp.dot(a, _bf16(w3_ref[...]), preferred_element_type=F32)
    o_ref[...] = (g * jax.nn.sigmoid(g) * u).astype(o_ref.dtype)

    @pl.when(pl.program_id(0) == 0)
    def _():
        w2b_ref[...] = w2_ref[...].astype(BF16)

    if mod_tiles:
        c_ref, wm_ref, bm_ref = refs[4:7]

        @pl.when(pl.program_id(0) * pl.num_programs(1) + pl.program_id(1) < mod_tiles)
        def _():
            refs[-1][...] = _mod_rows(c_ref, wm_ref, bm_ref)


def _swiglu(a, w1, w3, w2, layer, tm, tn, mod=None):
    rows, k = a.shape
    f = w1.shape[2]
    d = w2.shape[2]
    ni, nj = rows // tm, f // tn

    def slab(i, j):
        return jnp.where(i == 0, j, nj - 1)

    in_specs = [pl.BlockSpec((tm, k), lambda i, j: (i, 0)),
                _weight_spec(w1, k, 0, layer, tn), _weight_spec(w3, k, 0, layer, tn),
                pl.BlockSpec((None, tn, d), lambda i, j: (layer, slab(i, j), 0))]
    out_shape = [jax.ShapeDtypeStruct((rows, f), BF16), jax.ShapeDtypeStruct((f, d), BF16)]
    out_specs = [pl.BlockSpec((tm, tn), lambda i, j: (i, j)),
                 pl.BlockSpec((tn, d), lambda i, j: (slab(i, j), 0))]
    args = [a, w1, w3, w2]
    mod_tiles = 0
    if mod is not None:
        cvec, w_mod, b_mod, ml = mod
        n_mod = w_mod.shape[2]
        per_step = -(-n_mod // (LANES * ni * nj))
        while (n_mod // LANES) % per_step:
            per_step += 1
        mod_tile = per_step * LANES
        mod_tiles = n_mod // mod_tile

        def tile(i, j):
            return jnp.minimum(i * nj + j, mod_tiles - 1)

        in_specs += [pl.BlockSpec(cvec.shape, lambda i, j: (0, 0)),
                     pl.BlockSpec((None, w_mod.shape[1], mod_tile), lambda i, j: (ml, 0, tile(i, j))),
                     pl.BlockSpec((None, 1, mod_tile), lambda i, j: (ml, 0, tile(i, j)))]
        out_shape.append(jax.ShapeDtypeStruct((SUBLANES, n_mod), F32))
        out_specs.append(pl.BlockSpec((SUBLANES, mod_tile), lambda i, j: (0, tile(i, j))))
        args += [cvec, w_mod, b_mod]
    return pl.pallas_call(
        functools.partial(_swiglu_kernel, mod_tiles=mod_tiles),
        out_shape=tuple(out_shape),
        grid=(ni, nj),
        in_specs=in_specs,
        out_specs=tuple(out_specs),
        compiler_params=_params("arbitrary", "arbitrary"),
    )(*args)


def _rope_tables(n_tok, dim):
    rows = n_tok // GRID_W
    row = np.repeat(np.arange(rows, dtype=np.float32), GRID_W)
    col = np.tile(np.arange(GRID_W, dtype=np.float32), rows)
    axis_dim = dim // 2
    expo = -np.arange(0, axis_dim, 2, dtype=np.float32) / np.float32(axis_dim)
    inv = np.power(np.float32(ROPE_BASE), expo).astype(np.float32)
    ang_r = row[:, None] * inv[None, :]
    ang_c = col[:, None] * inv[None, :]
    ang = np.concatenate([ang_r, ang_r, ang_c, ang_c], axis=-1).astype(np.float32)
    return np.cos(ang), np.sin(ang)


def _rot_matrix(dim):
    q = dim // 4
    p = np.zeros((dim, dim), np.float32)
    eye = np.eye(q, dtype=np.float32)
    p[q:2 * q, 0:q] = -eye
    p[0:q, q:2 * q] = eye
    p[3 * q:, 2 * q:3 * q] = -eye
    p[2 * q:3 * q, 3 * q:] = eye
    return p


def _embed(block, shape, r0, c0, fill):
    out = np.full(shape, fill, np.float32)
    out[r0:r0 + block.shape[0], c0:c0 + block.shape[1]] = block
    return out


def _rope(x_bf16, cos, sin, rot):
    turned = jnp.dot(x_bf16, rot, preferred_element_type=F32)
    return x_bf16.astype(F32) * cos + turned * sin


KEY_CHUNK = 1024


def _softmax_values(scores, vones):
    keys = scores[0].shape[1]
    maxes = [jnp.max(s, axis=-1, keepdims=True) for s in scores]
    accs = [None] * len(scores)
    for c0 in range(0, keys, KEY_CHUNK):
        c1 = min(c0 + KEY_CHUNK, keys)
        for i, (s, m, vone) in enumerate(zip(scores, maxes, vones)):
            e = jnp.exp2(s[:, c0:c1] - m).astype(BF16)
            part = jnp.dot(e, vone[c0:c1, :], preferred_element_type=F32)
            accs[i] = part if accs[i] is None else accs[i] + part
    dv = accs[0].shape[1] // 2
    return [a[:, :dv] / a[:, dv:] for a in accs]


def _const_spec(shape):
    return pl.BlockSpec(shape, lambda bi, h, i: (0,) * len(shape), pipeline_mode=pl.Buffered(1))


_NT = (((1,), (1,)), ((), ()))


def _mla_kernel(*refs, tiles, n_ctx, scale, heads):
    q_refs, t_refs = refs[:tiles], refs[tiles:2 * tiles]
    kv_ref, kr_ref, ck_ref, sk_ref, pk_ref, o_ref, kcat_ref, vone_ref = refs[2 * tiles:]
    step = pl.program_id(2)
    hp = MLA_HEAD_PAD
    tq = q_refs[0].shape[0]

    @pl.when(step == 0)
    def _():
        kr = _rope(kr_ref[...], ck_ref[...], sk_ref[...], pk_ref[...]).astype(BF16)
        ones = jnp.ones((kr.shape[0], hp - MLA_V), BF16)
        for h in range(heads):
            kcat_ref[h, :, :MLA_NOPE] = kv_ref[:, h * hp:h * hp + MLA_NOPE]
            kcat_ref[h, :, MLA_NOPE:] = kr
            vone_ref[h, :, :MLA_V] = kv_ref[:, h * hp + MLA_NOPE:(h + 1) * hp]
            vone_ref[h, :, MLA_V:] = ones

    def attend(n_tiles, rows):
        scores = []
        for r in range(n_tiles):
            for h in range(heads):
                q = q_refs[r][:, h * hp:(h + 1) * hp].astype(F32) * t_refs[r][...] * scale
                scores.append(lax.dot_general(q.astype(BF16), kcat_ref[h, :rows, :], _NT,
                                              preferred_element_type=F32))
        outs = _softmax_values(scores, [vone_ref.at[h, :rows, :] for _ in range(n_tiles) for h in range(heads)])
        for r in range(n_tiles):
            for h in range(heads):
                o_ref[r * tq:(r + 1) * tq, h * MLA_V:(h + 1) * MLA_V] = outs[r * heads + h].astype(o_ref.dtype)

    @pl.when(step == 0)
    def _():
        attend(1, n_ctx)

    @pl.when(step > 0)
    def _():
        attend(tiles, kcat_ref.shape[1])


MLA_TILES_PER_STEP = 2


def _mla_attention(q, kv, hin, kr_block, tabs, n_ctx):
    b, t, _ = q.shape
    tq = n_ctx
    tq_tab, ck, sk, pk = tabs
    scale = (MLA_NOPE + MLA_ROPE) ** -0.5 * LOG2E
    hp = MLA_HEAD_PAD
    hs = MLA_HEADS_PER_STEP
    tiles = MLA_TILES_PER_STEP
    n_lat = (t - n_ctx) // tq
    assert hp == 2 * MLA_V and hp == MLA_NOPE + 2 * MLA_ROPE == MLA_NOPE + LANES and n_lat % tiles == 0

    def q_tile(s, r):
        return jnp.where(s == 0, r, tiles * (s - 1) + 1 + r)

    def out_block(s):
        return jnp.where(s == 0, n_lat // tiles, s - 1)

    q_specs = [pl.BlockSpec((None, tq, hs * hp), functools.partial(lambda bi, h, s, r: (bi, q_tile(s, r), h), r=r))
               for r in range(tiles)]
    t_specs = [pl.BlockSpec((tq, hp), functools.partial(lambda bi, h, s, r: (q_tile(s, r), 0), r=r))
               for r in range(tiles)]
    return pl.pallas_call(
        functools.partial(_mla_kernel, tiles=tiles, n_ctx=n_ctx, scale=scale, heads=hs),
        out_shape=jax.ShapeDtypeStruct((b, t, MLA_HEADS * MLA_V), BF16),
        grid=(b, MLA_HEADS // hs, 1 + n_lat // tiles),
        in_specs=q_specs + t_specs + [
            pl.BlockSpec((None, t, hs * hp), lambda bi, h, s: (bi, 0, h)),
            pl.BlockSpec((None, t, LANES), lambda bi, h, s: (bi, 0, kr_block)),
            _const_spec((t, LANES)), _const_spec((t, LANES)), _const_spec((LANES, LANES))],
        out_specs=pl.BlockSpec((None, tiles * tq, hs * MLA_V), lambda bi, h, s: (bi, out_block(s), h)),
        scratch_shapes=[pltpu.VMEM((hs, t, hp), BF16), pltpu.VMEM((hs, t, hp), BF16)],
        compiler_params=_params("parallel", "parallel", "arbitrary"),
    )(*([q] * tiles), *([tq_tab] * tiles), kv, hin, ck, sk, pk)


def _rotated_tile(i, n_tiles):
    return jnp.where(i == 0, n_tiles - 1, i - 1)


DIFF_TILES_PER_STEP = 2


def _diff_kernel(*refs, tiles, scale, lam_init):
    q_refs = refs[:tiles]
    k_ref, v_ref, cq_ref, sq_ref, ck_ref, sk_ref, p_ref, lam_ref, g_ref, o_ref, kro_ref, vone_ref = refs[tiles:]
    tq = q_refs[0].shape[0]

    @pl.when(pl.program_id(2) == 0)
    def _():
        kro_ref[...] = _rope(k_ref[...], ck_ref[...], sk_ref[...], p_ref[...]).astype(BF16)
        vone_ref[:, :DIFF_V] = v_ref[...]
        vone_ref[:, DIFF_V:] = jnp.ones((v_ref.shape[0], DIFF_V), BF16)

    lv = lam_ref[...]
    lam = (jnp.exp(jnp.sum(lv[0:1] * lv[1:2], axis=-1, keepdims=True))
           - jnp.exp(jnp.sum(lv[2:3] * lv[3:4], axis=-1, keepdims=True)) + lam_init)

    def scores(qm):
        return lax.dot_general(qm.astype(BF16), kro_ref[...], _NT, preferred_element_type=F32)

    chains = []
    for r, q_ref in enumerate(q_refs):
        rows = slice(r * tq, (r + 1) * tq)
        q = _rope(q_ref[...], cq_ref[rows, :], sq_ref[rows, :], p_ref[...]) * scale
        lane = lax.broadcasted_iota(jnp.int32, q.shape, 1)
        chains += [scores(jnp.where(lane < DIFF_QK, q, 0.0)), scores(jnp.where(lane >= DIFF_QK, q, 0.0))]
    outs = _softmax_values(chains, [vone_ref] * len(chains))
    for r in range(tiles):
        o = outs[2 * r] - lam * outs[2 * r + 1]
        norm = lax.rsqrt(jnp.mean(o * o, axis=-1, keepdims=True) + EPS)
        o_ref[r * tq:(r + 1) * tq, :] = (o * norm * g_ref[...] * (1.0 - lam_init)).astype(o_ref.dtype)


def _diff_attention(hin, tabs, lam_vecs, g_subln, lam_init, n_ctx):
    b, t, _ = hin.shape
    tq = n_ctx
    n = t - n_ctx
    cos, sin, p = tabs
    h_all = DIFF_HEADS
    tiles = DIFF_TILES_PER_STEP
    off = n_ctx // tq
    q_specs = [pl.BlockSpec((None, tq, LANES), functools.partial(lambda bi, h, i, r: (bi, tiles * i + off + r, h), r=r))
               for r in range(tiles)]
    q_tab = pl.BlockSpec((tiles * tq, LANES), lambda bi, h, i: (i, 0))
    return pl.pallas_call(
        functools.partial(_diff_kernel, tiles=tiles, scale=DIFF_QK ** -0.5 * LOG2E, lam_init=lam_init),
        out_shape=jax.ShapeDtypeStruct((b, n, h_all * DIFF_V), BF16),
        grid=(b, h_all, n // (tiles * tq)),
        in_specs=q_specs + [
            pl.BlockSpec((None, t, LANES), lambda bi, h, i: (bi, 0, h_all + h)),
            pl.BlockSpec((None, t, LANES), lambda bi, h, i: (bi, 0, 2 * h_all + h)),
            q_tab, q_tab,
            _const_spec((t, LANES)), _const_spec((t, LANES)),
            _const_spec((LANES, LANES)), _const_spec((4, DIFF_QK)), _const_spec((1, DIFF_V))],
        out_specs=pl.BlockSpec((None, tiles * tq, DIFF_V), lambda bi, h, i: (bi, i, h)),
        scratch_shapes=[pltpu.VMEM((t, LANES), BF16), pltpu.VMEM((t, 2 * DIFF_V), BF16)],
        compiler_params=_params("parallel", "parallel", "arbitrary"),
    )(*([hin] * tiles), hin, hin, cos[n_ctx:], sin[n_ctx:], cos, sin, p, lam_vecs,
      g_subln.reshape(1, DIFF_V).astype(F32))


def _s5_kernel(uf_ref, ub_ref, wb_ref, wc_ref, are_ref, aim_ref, yf_ref, yb_ref, s_ref, hs_ref, h_ref,
               *, tc):
    half = S5_GROUPS // S5_BLOCKS * S5_STATE
    gw = S5_GROUPS // S5_BLOCKS * S5_GROUP
    nl = half // LANES
    u_refs = (uf_ref, ub_ref)
    y_refs = (yf_ref, yb_ref)

    @pl.when(pl.program_id(1) == 0)
    def _():
        h_ref[...] = jnp.zeros_like(h_ref)

    for d in range(2):
        for j in range(S5_BLOCKS):
            bu = jnp.dot(u_refs[d][:, j * gw:(j + 1) * gw], wb_ref[d, j], preferred_element_type=F32)
            for c in range(2 * nl):
                s_ref.at[d, c][pl.ds(j, tc, stride=S5_BLOCKS), :] = bu[:, c * LANES:(c + 1) * LANES]

    a_re = [[are_ref[d, :, c * LANES:(c + 1) * LANES] for c in range(nl)] for d in range(2)]
    a_im = [[aim_ref[d, :, c * LANES:(c + 1) * LANES] for c in range(nl)] for d in range(2)]

    def step(t, h):
        rows = (t * S5_BLOCKS, (tc - 1 - t) * S5_BLOCKS)
        out = []
        for d in range(2):
            row = pl.multiple_of(rows[d], S5_BLOCKS)
            new = [None] * (2 * nl)
            for c in range(nl):
                h_re, h_im = h[d * 2 * nl + c], h[d * 2 * nl + nl + c]
                new[c] = a_re[d][c] * h_re - a_im[d][c] * h_im + s_ref[d, c, pl.ds(row, S5_BLOCKS), :]
                new[nl + c] = (a_re[d][c] * h_im + a_im[d][c] * h_re
                               + s_ref[d, nl + c, pl.ds(row, S5_BLOCKS), :])
            for c in range(2 * nl):
                hs_ref[d, c, pl.ds(row, S5_BLOCKS), :] = new[c]
            out += new
        return tuple(out)

    init = tuple(h_ref[d, c] for d in range(2) for c in range(2 * nl))
    h = lax.fori_loop(0, tc, step, init, unroll=4)
    for d in range(2):
        for c in range(2 * nl):
            h_ref[d, c] = h[d * 2 * nl + c]

    for d in range(2):
        for j in range(S5_BLOCKS):
            hj = jnp.concatenate([hs_ref.at[d, c][pl.ds(j, tc, stride=S5_BLOCKS), :] for c in range(2 * nl)],
                                 axis=1).astype(BF16)
            y_refs[d][:, j * gw:(j + 1) * gw] = jnp.dot(hj, wc_ref[d, j], preferred_element_type=F32)


def _s5_scan(hin, u_block, wb, wc, a_re, a_im, tc):
    b, t, _ = hin.shape
    width = S5_GROUPS * S5_GROUP
    half = S5_GROUPS // S5_BLOCKS * S5_STATE
    nchunk = t // tc

    def back(k):
        return jnp.where(k == 0, 0, nchunk - k)

    full = lambda a: pl.BlockSpec(a.shape, lambda bi, k: (0,) * a.ndim)
    out = jax.ShapeDtypeStruct((b, t, width), F32)
    return pl.pallas_call(
        functools.partial(_s5_kernel, tc=tc),
        out_shape=(out, out),
        grid=(b, nchunk),
        in_specs=[pl.BlockSpec((None, tc, width), lambda bi, k: (bi, k, u_block)),
                  pl.BlockSpec((None, tc, width), lambda bi, k: (bi, back(k), u_block)),
                  full(wb), full(wc), full(a_re), full(a_im)],
        out_specs=(pl.BlockSpec((None, tc, width), lambda bi, k: (bi, k, 0)),
                   pl.BlockSpec((None, tc, width), lambda bi, k: (bi, back(k), 0))),
        scratch_shapes=[pltpu.VMEM((2, 2 * half // LANES, tc * S5_BLOCKS, LANES), F32),
                        pltpu.VMEM((2, 2 * half // LANES, tc * S5_BLOCKS, LANES), F32),
                        pltpu.VMEM((2, 2 * half // LANES, S5_BLOCKS, LANES), F32)],
        compiler_params=_params("parallel", "arbitrary"),
    )(hin, hin, wb, wc, a_re, a_im)


def _s5_weights(lam_re, lam_im, log_dt, b_re, b_im, c_re, c_im):
    lr = lam_re.astype(F32)
    li = lam_im.astype(F32)
    dt = jnp.exp(log_dt.astype(F32))[..., None]
    mag = jnp.exp(lr * dt)
    ab_re = mag * jnp.cos(li * dt)
    ab_im = mag * jnp.sin(li * dt)
    den = lr * lr + li * li
    num_re = ab_re - 1.0
    coef_re = ((num_re * lr + ab_im * li) / den)[..., None]
    coef_im = ((ab_im * lr - num_re * li) / den)[..., None]
    br = b_re.astype(F32)
    bi = b_im.astype(F32)
    bb_re = coef_re * br - coef_im * bi
    bb_im = coef_re * bi + coef_im * br
    nb = S5_BLOCKS
    gb = S5_GROUPS // nb
    eye = jnp.eye(gb, dtype=F32)

    def drive(bb):
        w = jnp.einsum('djgpi,gh->djgihp', bb.reshape(2, nb, gb, S5_STATE, S5_GROUP), eye)
        return w.reshape(2, nb, gb * S5_GROUP, gb * S5_STATE)

    def read(c):
        w = jnp.einsum('djgip,gh->djgphi', c.astype(F32).reshape(2, nb, gb, S5_GROUP, S5_STATE), eye)
        return w.reshape(2, nb, gb * S5_STATE, gb * S5_GROUP)

    wb = jnp.concatenate([drive(bb_re), drive(bb_im)], axis=-1).astype(BF16)
    wc = jnp.concatenate([read(c_re), -read(c_im)], axis=-2).astype(BF16)
    return wb, wc, ab_re.reshape(2, nb, gb * S5_STATE), ab_im.reshape(2, nb, gb * S5_STATE)


def _s5_glu_kernel(yf_ref, yb_ref, u_ref, d_ref, w_ref, o_ref):
    y = yf_ref[...] + yb_ref[...] + d_ref[...] * u_ref[...].astype(F32)
    r = jnp.dot(jax.nn.gelu(y).astype(BF16), w_ref[...], preferred_element_type=F32)
    n = r.shape[1] // 2
    o_ref[...] = (r[:, :n] * jax.nn.sigmoid(r[:, n:])).astype(o_ref.dtype)


def _s5_glu(yf, yb, hin, u_block, d_skip, w_glu, tm):
    b, t, width = yf.shape
    return pl.pallas_call(
        _s5_glu_kernel,
        out_shape=jax.ShapeDtypeStruct((b, t, width), BF16),
        grid=(b, t // tm),
        in_specs=[pl.BlockSpec((None, tm, width), lambda bi, i: (bi, i, 0)),
                  pl.BlockSpec((None, tm, width), lambda bi, i: (bi, i, 0)),
                  pl.BlockSpec((None, tm, width), lambda bi, i: (bi, i, u_block)),
                  pl.BlockSpec((1, width), lambda bi, i: (0, 0)),
                  pl.BlockSpec(w_glu.shape, lambda bi, i: (0, 0))],
        out_specs=pl.BlockSpec((None, tm, width), lambda bi, i: (bi, _rotated_tile(i, t // tm), 0)),
        compiler_params=_params("parallel", "parallel"),
    )(yf, yb, hin, d_skip.reshape(1, width).astype(F32), w_glu)


CONV_HALO = 16


def _conv_kernel(ap_ref, gp_ref, ac_ref, gc_ref, an_ref, gn_ref, w_ref, b_ref, lg_ref, lb_ref,
                 o_ref, v_ref, sh_ref, y_ref, *, tr, taps):
    i = pl.program_id(1)
    last = pl.num_programs(1) - 1

    def glu(a_ref, g_ref):
        return a_ref[...].astype(F32) * jax.nn.sigmoid(g_ref[...].astype(F32))

    v_ref[0:CONV_HALO, :] = jnp.where(i > 0, glu(ap_ref, gp_ref), 0.0)
    v_ref[CONV_HALO:CONV_HALO + tr, :] = glu(ac_ref, gc_ref)
    v_ref[CONV_HALO + tr:, :] = jnp.where(i < last, glu(an_ref, gn_ref), 0.0)

    span = v_ref.shape[0] - SUBLANES
    for r in range(1, SUBLANES):
        sh_ref[r - 1, 0:span, :] = v_ref[r:r + span, :]

    first = CONV_HALO - taps // 2
    for c0 in range(0, v_ref.shape[1], LANES):
        acc = jnp.zeros((tr, LANES), F32)
        for k in range(taps):
            tile, r = divmod(first + k, SUBLANES)
            src = v_ref if r == 0 else sh_ref.at[r - 1]
            window = src[tile * SUBLANES:tile * SUBLANES + tr, c0:c0 + LANES]
            acc = acc + w_ref[k:k + 1, c0:c0 + LANES] * window
        y_ref[:, c0:c0 + LANES] = acc

    y = y_ref[...] + b_ref[...]
    yc = y - jnp.mean(y, axis=-1, keepdims=True)
    yn = yc * lax.rsqrt(jnp.mean(yc * yc, axis=-1, keepdims=True) + EPS) * lg_ref[...] + lb_ref[...]
    o_ref[...] = (yn * jax.nn.sigmoid(yn)).astype(o_ref.dtype)


def _conformer_conv(hin, a_block, w_dw, b_dw, g_n, b_n, n_ctx, tr):
    b, t, _ = hin.shape
    n = t - n_ctx
    taps, ch = w_dw.shape
    hb = tr // CONV_HALO
    off = n_ctx // tr
    last_halo = t // CONV_HALO - 1

    def prev_map(cb):
        return lambda bi, i: (bi, (i + off) * hb - 1, cb)

    def next_map(cb):
        return lambda bi, i: (bi, jnp.minimum((i + off + 1) * hb, last_halo), cb)

    def cur_map(cb):
        return lambda bi, i: (bi, i + off, cb)

    vec = lambda v: v.reshape(1, ch).astype(F32)
    halo = lambda m: pl.BlockSpec((None, CONV_HALO, ch), m)
    full = lambda shape: pl.BlockSpec(shape, lambda bi, i: (0, 0))
    return pl.pallas_call(
        functools.partial(_conv_kernel, tr=tr, taps=taps),
        out_shape=jax.ShapeDtypeStruct((b, n, ch), BF16),
        grid=(b, n // tr),
        in_specs=[halo(prev_map(a_block)), halo(prev_map(a_block + 1)),
                  pl.BlockSpec((None, tr, ch), cur_map(a_block)),
                  pl.BlockSpec((None, tr, ch), cur_map(a_block + 1)),
                  halo(next_map(a_block)), halo(next_map(a_block + 1)),
                  full((taps, ch)), full((1, ch)), full((1, ch)), full((1, ch))],
        out_specs=pl.BlockSpec((None, tr, ch), lambda bi, i: (bi, i, 0)),
        scratch_shapes=[pltpu.VMEM((tr + 2 * CONV_HALO, ch), F32),
                        pltpu.VMEM((SUBLANES - 1, tr + 2 * CONV_HALO, ch), F32),
                        pltpu.VMEM((tr, ch), F32)],
        compiler_params=_params("parallel", "parallel"),
    )(hin, hin, hin, hin, hin, hin, w_dw.astype(F32), vec(b_dw), vec(g_n), vec(b_n))


def _ffn_matmuls(h, w1, w3, w2, layer, mod=None):
    b, rows, d = h.shape
    outs = _swiglu(h.reshape(b * rows, d), w1, w3, w2, layer, _row_tile(b * rows, 1100), 256, mod)
    act, w2b = outs[:2]
    y = _matmul([(act, 0, w2b, 0, w2b.shape[0])], _row_tile(b * rows, 550), 512)
    return y.reshape(b, rows, d), (outs[2] if mod is not None else None)


def kernel(x, c, ctx, c_ctx, w_mod, b_mod, g_mix_pre, g_mix_post, g_ffn_pre, g_ffn_post, w_ff1, w_ff3, w_ff2, e_w_in, e_g_q, e_w_uq, e_g_kv, e_w_ukv, e_s5_lam_re, e_s5_lam_im, e_s5_log_dt, e_s5_b_re, e_s5_b_im, e_s5_c_re, e_s5_c_im, e_s5_d, e_w_glu, e_w_out, o_w_in, o_lam_q1, o_lam_k1, o_lam_q2, o_lam_k2, o_g_subln, o_conv_w, o_conv_b, o_conv_norm_g, o_conv_norm_b, o_w_out):
    b, n, d = x.shape
    n_ctx = ctx.shape[1]
    t = n_ctx + n
    depth = w_mod.shape[0]
    assert depth == 2 and e_w_in.shape[0] == 1 and o_w_in.shape[0] == 1
    assert b + 1 <= SUBLANES and n % n_ctx == 0
    tr = n_ctx
    q_rank = e_g_q.shape[1]
    kv_rank = e_g_kv.shape[1]
    s5_w = S5_GROUPS * S5_GROUP

    cvec = jnp.zeros((SUBLANES, d), F32).at[:b].set(c).at[b].set(c_ctx)
    b_mod3 = b_mod.reshape(depth, 1, N_MOD * d)

    def modtab(rows):
        lat = rows[:b].reshape(b, 1, N_MOD, d)
        cx = jnp.broadcast_to(rows[b].reshape(1, 1, N_MOD, d), (b, 1, N_MOD, d))
        return jnp.concatenate([cx, lat], axis=1)

    modtabs = [modtab(_modulation(cvec, w_mod, b_mod3, 1)[0]), None]

    cos, sin = _rope_tables(n, MLA_ROPE)
    rot = _rot_matrix(MLA_ROPE)
    hp = MLA_HEAD_PAD
    cos2 = np.concatenate([cos, cos], axis=1)
    sin2 = np.concatenate([sin, sin], axis=1)
    rot2 = _embed(rot, (LANES, LANES), 0, 0, 0.0)
    rot2[DIFF_QK:, DIFF_QK:] = rot
    pair_tabs = (_embed(cos2, (t, LANES), n_ctx, 0, 1.0), _embed(sin2, (t, LANES), n_ctx, 0, 0.0),
                 jnp.asarray(rot2, BF16))
    q_tab = _embed(np.concatenate([cos, sin], axis=1), (t, hp), n_ctx, MLA_NOPE, 1.0)
    q_tab[:n_ctx, MLA_NOPE + MLA_ROPE:] = 0.0

    lw = e_w_in[0]
    cuts = [q_rank, q_rank + kv_rank, q_rank + kv_rank + MLA_ROPE]
    w_in = jnp.concatenate([lw[:, :cuts[0]], lw[:, cuts[2]:], lw[:, cuts[0]:cuts[1]], lw[:, cuts[1]:cuts[2]],
                            lw[:, cuts[1]:cuts[2]]], axis=1).astype(BF16)
    u_block = q_rank // s5_w
    kv_block = (q_rank + s5_w) // kv_rank
    kr_block = (q_rank + s5_w + kv_rank) // LANES
    w_uq = e_w_uq[0].reshape(q_rank, MLA_HEADS, MLA_NOPE + MLA_ROPE)
    w_uq = jnp.concatenate([w_uq, jnp.einsum('khj,ji->khi', w_uq[:, :, MLA_NOPE:], rot)], axis=2)
    w_uq = w_uq.reshape(q_rank, MLA_HEADS * hp).astype(BF16)
    n_attn = MLA_HEADS * MLA_V

    h = _norm_mod((ctx, x), g_mix_pre[0], modtabs[0], SH_M, SC_M, tr).reshape(b * t, d)
    tm = _row_tile(b * t, 1100)
    hin = _matmul([(h, 0, w_in, 0, d)], tm, w_in.shape[1] // 3)
    q = _matmul([(hin, 0, w_uq, 0, q_rank)], tm, 2048, gain=e_g_q[0])
    kv = _matmul([(hin, kv_block, e_w_ukv, 0, kv_rank)], tm, 2048, gain=e_g_kv[0])
    hin3 = hin.reshape(b, t, -1)
    attn = _mla_attention(q.reshape(b, t, -1), kv.reshape(b, t, -1), hin3, kr_block,
                          (q_tab,) + pair_tabs, n_ctx)
    wb, wc, a_re, a_im = _s5_weights(e_s5_lam_re[0], e_s5_lam_im[0], e_s5_log_dt[0], e_s5_b_re[0],
                                     e_s5_b_im[0], e_s5_c_re[0], e_s5_c_im[0])
    yf, yb = _s5_scan(hin3, u_block, wb, wc, a_re, a_im, tr)
    s5 = _s5_glu(yf, yb, hin3, u_block, e_s5_d[0], e_w_glu[0].astype(BF16), tr)
    y = _matmul([(attn.reshape(b * t, -1), 0, e_w_out, 0, n_attn),
                 (s5.reshape(b * t, -1), 0, e_w_out, n_attn // s5_w, s5_w)], tm, 512)
    z, h = _residual((ctx, x), y.reshape(b, t, d), g_mix_post[0], modtabs[0], GT_M, 0, 0, tr,
                     nxt=(g_ffn_pre[0], modtabs[0], SH_F, SC_F), y_rotated=True)
    y, mod1 = _ffn_matmuls(h, w_ff1, w_ff3, w_ff2, 0, mod=(cvec, w_mod, b_mod3, 1))
    modtabs[1] = modtab(mod1)
    z, h = _residual(z, y, g_ffn_post[0], modtabs[0], GT_F, 0, 0, tr,
                     nxt=(g_mix_pre[1], modtabs[1], SH_M, SC_M))

    lam_init = 0.8 - 0.6 * math.exp(-0.3 * 1)
    n_attn = DIFF_HEADS * DIFF_V
    conv_ch = o_conv_w.shape[2]

    hin3 = _matmul([(h.reshape(b * t, d), 0, o_w_in, 0, d)], tm, 512).reshape(b, t, -1)
    lam_vecs = jnp.stack([o_lam_q1[0], o_lam_k1[0], o_lam_q2[0], o_lam_k2[0]]).astype(F32)
    attn = _diff_attention(hin3, pair_tabs, lam_vecs, o_g_subln[0], lam_init, n_ctx)
    conv = _conformer_conv(hin3, 3 * n_attn // conv_ch, o_conv_w[0], o_conv_b[0], o_conv_norm_g[0],
                           o_conv_norm_b[0], n_ctx, tr)
    tml = _row_tile(b * n, 1100)
    y = _matmul([(attn.reshape(b * n, -1), 0, o_w_out, 0, n_attn),
                 (conv.reshape(b * n, -1), 0, o_w_out, n_attn // conv_ch, conv_ch)], tml, 512)
    xl, h = _residual(z, y.reshape(b, n, d), g_mix_post[1], modtabs[1], GT_M, n_ctx // tr, 1, tr,
                      nxt=(g_ffn_pre[1], modtabs[1], SH_F, SC_F))
    y, _ = _ffn_matmuls(h, w_ff1, w_ff3, w_ff2, 1)
    return _residual(xl, y, g_ffn_post[1], modtabs[1], GT_F, 0, 1, tr)
```

```python
import functools
import math

import numpy as np

import jax
import jax.numpy as jnp
from jax import lax
from jax.experimental import pallas as pl
from jax.experimental.pallas import tpu as pltpu

F32 = jnp.float32
BF16 = jnp.bfloat16

EPS = 1e-6
ROPE_BASE = 10000.0
GRID_W = 64
LOG2E = math.log2(math.e)

MLA_HEADS = 16
MLA_NOPE = 128
MLA_ROPE = 64
MLA_V = 128
MLA_HEAD_PAD = 256
MLA_HEADS_PER_STEP = 2
DIFF_HEADS = 16
DIFF_QK = 64
DIFF_V = 128
S5_GROUP = 16
S5_GROUPS = 64
S5_STATE = 64
S5_BLOCKS = 8
N_MOD = 6

LANES = 128
SUBLANES = 8
VMEM_LIMIT = 56 * 1024 * 1024

ROW_TILE_CAP = 1100
LONG_K_ROW_TILE_CAP = 550
FF_COL_TILE = 256
DOWN_COL_TILE = 512
PROJ_COL_TILE = 512
UP_COL_TILE = 2048
MOD_COL_TILE = 512

SH_M, SC_M, GT_M, SH_F, SC_F, GT_F = range(N_MOD)


def _params(*sem):
    return pltpu.CompilerParams(dimension_semantics=sem, vmem_limit_bytes=VMEM_LIMIT)


def _row_tile(rows, cap):
    k = -(-rows // cap)
    while rows % k or (rows // k) % 16:
        k += 1
    return rows // k


def _mod_rows(c_ref, w_ref, b_ref):
    c = c_ref[...]
    a = (c * jax.nn.sigmoid(c)).astype(BF16)
    return jnp.dot(a, w_ref[...].astype(BF16), preferred_element_type=F32) + b_ref[...]


def _mod_kernel(c_ref, w_ref, b_ref, o_ref):
    o_ref[...] = _mod_rows(c_ref, w_ref, b_ref)


def _modulation(cvec, w_mod, b_mod, layers):
    _, d, n = w_mod.shape
    tn = MOD_COL_TILE
    return pl.pallas_call(
        _mod_kernel,
        out_shape=jax.ShapeDtypeStruct((layers, SUBLANES, n), F32),
        grid=(layers, n // tn),
        in_specs=[pl.BlockSpec((SUBLANES, d), lambda l, j: (0, 0)),
                  pl.BlockSpec((None, d, tn), lambda l, j: (l, 0, j)),
                  pl.BlockSpec((None, 1, tn), lambda l, j: (l, 0, j))],
        out_specs=pl.BlockSpec((None, SUBLANES, tn), lambda l, j: (l, 0, j)),
        compiler_params=_params("parallel", "parallel"),
    )(cvec, w_mod, b_mod)


def _stream_specs(src, tr, d, tile_offset):
    if isinstance(src, tuple):
        return ([pl.BlockSpec((None, tr, d), lambda bi, i: (bi, 0, 0)),
                 pl.BlockSpec((None, tr, d), lambda bi, i: (bi, jnp.maximum(i - 1, 0), 0))], list(src))
    return [pl.BlockSpec((None, tr, d), lambda bi, i: (bi, i + tile_offset, 0))], [src]


def _stream_rows(x_refs):
    if len(x_refs) == 1:
        return x_refs[0][...]
    return jnp.where(pl.program_id(1) == 0, x_refs[0][...], x_refs[1][...])


def _mod_spec(d, seg_offset):
    return pl.BlockSpec((None, None, N_MOD, d), lambda bi, i: (bi, jnp.minimum(i + seg_offset, 1), 0, 0))


def _adaln(x, g_ref, m_ref, shift_row, scale_row):
    r = lax.rsqrt(jnp.mean(x * x, axis=-1, keepdims=True) + EPS)
    h = x * r * g_ref[...]
    return (h * (1.0 + m_ref[scale_row:scale_row + 1, :]) + m_ref[shift_row:shift_row + 1, :]).astype(BF16)


def _norm_mod_kernel(*refs, n_src, shift_row, scale_row):
    g_ref, m_ref, o_ref = refs[n_src:]
    o_ref[...] = _adaln(_stream_rows(refs[:n_src]), g_ref, m_ref, shift_row, scale_row)


def _norm_mod(src, gain, modtab, shift_row, scale_row, tr):
    specs, args = _stream_specs(src, tr, gain.shape[0], 0)
    b = args[0].shape[0]
    d = gain.shape[0]
    rows = sum(a.shape[1] for a in args)
    return pl.pallas_call(
        functools.partial(_norm_mod_kernel, n_src=len(args), shift_row=shift_row, scale_row=scale_row),
        out_shape=jax.ShapeDtypeStruct((b, rows, d), BF16),
        grid=(b, rows // tr),
        in_specs=specs + [pl.BlockSpec((1, d), lambda bi, i: (0, 0)), _mod_spec(d, 0)],
        out_specs=pl.BlockSpec((None, tr, d), lambda bi, i: (bi, i, 0)),
        compiler_params=_params("parallel", "parallel"),
    )(*args, gain.reshape(1, d), modtab)


def _residual_kernel(*refs, n_src, gate_row, nxt_rows):
    y_ref, g_ref, m_ref = refs[n_src:n_src + 3]
    y = y_ref[...].astype(F32)
    r = lax.rsqrt(jnp.mean(y * y, axis=-1, keepdims=True) + EPS)
    z = _stream_rows(refs[:n_src]) + m_ref[gate_row:gate_row + 1, :] * (y * r * g_ref[...])
    if nxt_rows is None:
        refs[-1][...] = z
    else:
        g2_ref, m2_ref, o_ref, h_ref = refs[n_src + 3:]
        o_ref[...] = z
        h_ref[...] = _adaln(z, g2_ref, m2_ref, *nxt_rows)


def _residual(src, y, gain, modtab, gate_row, x_tile_offset, seg_offset, tr, nxt=None, y_rotated=False):
    b, rows, d = y.shape
    specs, args = _stream_specs(src, tr, d, x_tile_offset)
    tile = lambda: pl.BlockSpec((None, tr, d), lambda bi, i: (bi, i, 0))
    vec = lambda: pl.BlockSpec((1, d), lambda bi, i: (0, 0))
    y_spec = tile()
    if y_rotated:
        y_spec = pl.BlockSpec((None, tr, d), lambda bi, i: (bi, _rotated_tile(i, rows // tr), 0))
    in_specs = specs + [y_spec, vec(), _mod_spec(d, seg_offset)]
    args = args + [y, gain.reshape(1, d), modtab]
    out_shape = jax.ShapeDtypeStruct((b, rows, d), F32)
    out_specs = tile()
    nxt_rows = None
    if nxt is not None:
        in_specs += [vec(), _mod_spec(d, seg_offset)]
        args += [nxt[0].reshape(1, d), nxt[1]]
        nxt_rows = (nxt[2], nxt[3])
        out_shape = (out_shape, jax.ShapeDtypeStruct((b, rows, d), BF16))
        out_specs = (out_specs, tile())
    return pl.pallas_call(
        functools.partial(_residual_kernel, n_src=len(specs), gate_row=gate_row, nxt_rows=nxt_rows),
        out_shape=out_shape,
        grid=(b, rows // tr),
        in_specs=in_specs,
        out_specs=out_specs,
        compiler_params=_params("parallel", "parallel"),
    )(*args)


def _bf16(w):
    return w if w.dtype == BF16 else w.astype(BF16)


def _matmul_kernel(*refs, n_pairs, normed):
    o_ref = refs[-1]
    acc = None
    for p in range(n_pairs):
        a = refs[2 * p][...]
        if normed:
            af = a.astype(F32)
            r = lax.rsqrt(jnp.mean(af * af, axis=-1, keepdims=True) + EPS)
            a = (af * r * refs[2 * n_pairs][...]).astype(BF16)
        part = jnp.dot(a, _bf16(refs[2 * p + 1][...]), preferred_element_type=F32)
        acc = part if acc is None else acc + part
    o_ref[...] = acc.astype(o_ref.dtype)


def _weight_spec(w, k, row_block, layer, tn):
    if w.ndim == 3:
        return pl.BlockSpec((None, k, tn), lambda i, j: (layer, row_block, j))
    return pl.BlockSpec((k, tn), lambda i, j: (row_block, j))


def _matmul(pairs, tm, tn, gain=None, layer=0, out_dtype=BF16):
    rows = pairs[0][0].shape[0]
    n = pairs[0][2].shape[-1]
    tn = min(tn, n)
    in_specs, args = [], []
    for a, cb, w, rb, k in pairs:
        in_specs.append(pl.BlockSpec((tm, k), functools.partial(lambda i, j, cb: (i, cb), cb=cb)))
        in_specs.append(_weight_spec(w, k, rb, layer, tn))
        args += [a, w]
    if gain is not None:
        k = pairs[0][4]
        in_specs.append(pl.BlockSpec((1, k), lambda i, j: (0, 0)))
        args.append(gain.reshape(1, k).astype(F32))
    return pl.pallas_call(
        functools.partial(_matmul_kernel, n_pairs=len(pairs), normed=gain is not None),
        out_shape=jax.ShapeDtypeStruct((rows, n), out_dtype),
        grid=(rows // tm, n // tn),
        in_specs=in_specs,
        out_specs=pl.BlockSpec((tm, tn), lambda i, j: (i, j)),
        compiler_params=_params("parallel", "arbitrary"),
    )(*args)


def _swiglu_kernel(*refs, mod_tiles):
    a_ref, w1_ref, w3_ref, w2_ref = refs[:4]
    o_ref, w2b_ref = refs[-2 - (mod_tiles > 0):][:2]
    a = a_ref[...]
    g = jnp.dot(a, _bf16(w1_ref[...]), preferred_element_type=F32)
    u = jnp.dot(a, _bf16(w3_ref[...]), preferred_element_type=F32)
    o_ref[...] = (g * jax.nn.sigmoid(g) * u).astype(o_ref.dtype)

    @pl.when(pl.program_id(0) == 0)
    def _():
        w2b_ref[...] = w2_ref[...].astype(BF16)

    if mod_tiles:
        c_ref, wm_ref, bm_ref = refs[4:7]

        @pl.when(pl.program_id(0) * pl.num_programs(1) + pl.program_id(1) < mod_tiles)
        def _():
            refs[-1][...] = _mod_rows(c_ref, wm_ref, bm_ref)


def _swiglu(a, w1, w3, w2, layer, tm, tn, mod=None):
    rows, k = a.shape
    f = w1.shape[2]
    d = w2.shape[2]
    ni, nj = rows // tm, f // tn

    def slab(i, j):
        return jnp.where(i == 0, j, nj - 1)

    in_specs = [pl.BlockSpec((tm, k), lambda i, j: (i, 0)),
                _weight_spec(w1, k, 0, layer, tn), _weight_spec(w3, k, 0, layer, tn),
                pl.BlockSpec((None, tn, d), lambda i, j: (layer, slab(i, j), 0))]
    out_shape = [jax.ShapeDtypeStruct((rows, f), BF16), jax.ShapeDtypeStruct((f, d), BF16)]
    out_specs = [pl.BlockSpec((tm, tn), lambda i, j: (i, j)),
                 pl.BlockSpec((tn, d), lambda i, j: (slab(i, j), 0))]
    args = [a, w1, w3, w2]
    mod_tiles = 0
    if mod is not None:
        cvec, w_mod, b_mod, ml = mod
        n_mod = w_mod.shape[2]
        per_step = -(-n_mod // (LANES * ni * nj))
        while (n_mod // LANES) % per_step:
            per_step += 1
        mod_tile = per_step * LANES
        mod_tiles = n_mod // mod_tile

        def tile(i, j):
            return jnp.minimum(i * nj + j, mod_tiles - 1)

        in_specs += [pl.BlockSpec(cvec.shape, lambda i, j: (0, 0)),
                     pl.BlockSpec((None, w_mod.shape[1], mod_tile), lambda i, j: (ml, 0, tile(i, j))),
                     pl.BlockSpec((None, 1, mod_tile), lambda i, j: (ml, 0, tile(i, j)))]
        out_shape.append(jax.ShapeDtypeStruct((SUBLANES, n_mod), F32))
        out_specs.append(pl.BlockSpec((SUBLANES, mod_tile), lambda i, j: (0, tile(i, j))))
        args += [cvec, w_mod, b_mod]
    return pl.pallas_call(
        functools.partial(_swiglu_kernel, mod_tiles=mod_tiles),
        out_shape=tuple(out_shape),
        grid=(ni, nj),
        in_specs=in_specs,
        out_specs=tuple(out_specs),
        compiler_params=_params("arbitrary", "arbitrary"),
    )(*args)


def _rope_tables(n_tok, dim):
    rows = n_tok // GRID_W
    row = np.repeat(np.arange(rows, dtype=np.float32), GRID_W)
    col = np.tile(np.arange(GRID_W, dtype=np.float32), rows)
    axis_dim = dim // 2
    expo = -np.arange(0, axis_dim, 2, dtype=np.float32) / np.float32(axis_dim)
    inv = np.power(np.float32(ROPE_BASE), expo).astype(np.float32)
    ang_r = row[:, None] * inv[None, :]
    ang_c = col[:, None] * inv[None, :]
    ang = np.concatenate([ang_r, ang_r, ang_c, ang_c], axis=-1).astype(np.float32)
    return np.cos(ang), np.sin(ang)


def _rot_matrix(dim):
    q = dim // 4
    p = np.zeros((dim, dim), np.float32)
    eye = np.eye(q, dtype=np.float32)
    p[q:2 * q, 0:q] = -eye
    p[0:q, q:2 * q] = eye
    p[3 * q:, 2 * q:3 * q] = -eye
    p[2 * q:3 * q, 3 * q:] = eye
    return p


def _embed(block, shape, r0, c0, fill):
    out = np.full(shape, fill, np.float32)
    out[r0:r0 + block.shape[0], c0:c0 + block.shape[1]] = block
    return out


def _rope(x_bf16, cos, sin, rot):
    turned = jnp.dot(x_bf16, rot, preferred_element_type=F32)
    return x_bf16.astype(F32) * cos + turned * sin


KEY_CHUNK = 1024


def _softmax_values(scores, vones):
    keys = scores[0].shape[1]
    maxes = [jnp.max(s, axis=-1, keepdims=True) for s in scores]
    accs = [None] * len(scores)
    for c0 in range(0, keys, KEY_CHUNK):
        c1 = min(c0 + KEY_CHUNK, keys)
        for i, (s, m, vone) in enumerate(zip(scores, maxes, vones)):
            e = jnp.exp2(s[:, c0:c1] - m).astype(BF16)
            part = jnp.dot(e, vone[c0:c1, :], preferred_element_type=F32)
            accs[i] = part if accs[i] is None else accs[i] + part
    dv = accs[0].shape[1] // 2
    return [a[:, :dv] / a[:, dv:] for a in accs]


def _const_spec(shape):
    return pl.BlockSpec(shape, lambda bi, h, i: (0,) * len(shape), pipeline_mode=pl.Buffered(1))


_NT = (((1,), (1,)), ((), ()))


def _mla_kernel(*refs, tiles, n_ctx, scale, heads):
    q_refs, t_refs = refs[:tiles], refs[tiles:2 * tiles]
    kv_ref, kr_ref, ck_ref, sk_ref, pk_ref, o_ref, kcat_ref, vone_ref = refs[2 * tiles:]
    step = pl.program_id(2)
    hp = MLA_HEAD_PAD
    tq = q_refs[0].shape[0]

    @pl.when(step == 0)
    def _():
        kr = _rope(kr_ref[...], ck_ref[...], sk_ref[...], pk_ref[...]).astype(BF16)
        ones = jnp.ones((kr.shape[0], hp - MLA_V), BF16)
        for h in range(heads):
            kcat_ref[h, :, :MLA_NOPE] = kv_ref[:, h * hp:h * hp + MLA_NOPE]
            kcat_ref[h, :, MLA_NOPE:] = kr
            vone_ref[h, :, :MLA_V] = kv_ref[:, h * hp + MLA_NOPE:(h + 1) * hp]
            vone_ref[h, :, MLA_V:] = ones

    def attend(n_tiles, rows):
        scores = []
        for r in range(n_tiles):
            for h in range(heads):
                q = q_refs[r][:, h * hp:(h + 1) * hp].astype(F32) * t_refs[r][...] * scale
                scores.append(lax.dot_general(q.astype(BF16), kcat_ref[h, :rows, :], _NT,
                                              preferred_element_type=F32))
        outs = _softmax_values(scores, [vone_ref.at[h, :rows, :] for _ in range(n_tiles) for h in range(heads)])
        for r in range(n_tiles):
            for h in range(heads):
                o_ref[r * tq:(r + 1) * tq, h * MLA_V:(h + 1) * MLA_V] = outs[r * heads + h].astype(o_ref.dtype)

    @pl.when(step == 0)
    def _():
        attend(1, n_ctx)

    @pl.when(step > 0)
    def _():
        attend(tiles, kcat_ref.shape[1])


MLA_TILES_PER_STEP = 2


def _mla_attention(q, kv, hin, kr_block, tabs, n_ctx):
    b, t, _ = q.shape
    tq = n_ctx
    tq_tab, ck, sk, pk = tabs
    scale = (MLA_NOPE + MLA_ROPE) ** -0.5 * LOG2E
    hp = MLA_HEAD_PAD
    hs = MLA_HEADS_PER_STEP
    tiles = MLA_TILES_PER_STEP
    n_lat = (t - n_ctx) // tq
    assert hp == 2 * MLA_V and hp == MLA_NOPE + 2 * MLA_ROPE == MLA_NOPE + LANES and n_lat % tiles == 0

    def q_tile(s, r):
        return jnp.where(s == 0, r, tiles * (s - 1) + 1 + r)

    def out_block(s):
        return jnp.where(s == 0, n_lat // tiles, s - 1)

    q_specs = [pl.BlockSpec((None, tq, hs * hp), functools.partial(lambda bi, h, s, r: (bi, q_tile(s, r), h), r=r))
               for r in range(tiles)]
    t_specs = [pl.BlockSpec((tq, hp), functools.partial(lambda bi, h, s, r: (q_tile(s, r), 0), r=r))
               for r in range(tiles)]
    return pl.pallas_call(
        functools.partial(_mla_kernel, tiles=tiles, n_ctx=n_ctx, scale=scale, heads=hs),
        out_shape=jax.ShapeDtypeStruct((b, t, MLA_HEADS * MLA_V), BF16),
        grid=(b, MLA_HEADS // hs, 1 + n_lat // tiles),
        in_specs=q_specs + t_specs + [
            pl.BlockSpec((None, t, hs * hp), lambda bi, h, s: (bi, 0, h)),
            pl.BlockSpec((None, t, LANES), lambda bi, h, s: (bi, 0, kr_block)),
            _const_spec((t, LANES)), _const_spec((t, LANES)), _const_spec((LANES, LANES))],
        out_specs=pl.BlockSpec((None, tiles * tq, hs * MLA_V), lambda bi, h, s: (bi, out_block(s), h)),
        scratch_shapes=[pltpu.VMEM((hs, t, hp), BF16), pltpu.VMEM((hs, t, hp), BF16)],
        compiler_params=_params("parallel", "parallel", "arbitrary"),
    )(*([q] * tiles), *([tq_tab] * tiles), kv, hin, ck, sk, pk)


def _rotated_tile(i, n_tiles):
    return jnp.where(i == 0, n_tiles - 1, i - 1)


DIFF_TILES_PER_STEP = 2


def _diff_kernel(*refs, tiles, scale, lam_init):
    q_refs = refs[:tiles]
    k_ref, v_ref, cq_ref, sq_ref, ck_ref, sk_ref, p_ref, lam_ref, g_ref, o_ref, kro_ref, vone_ref = refs[tiles:]
    tq = q_refs[0].shape[0]

    @pl.when(pl.program_id(2) == 0)
    def _():
        kro_ref[...] = _rope(k_ref[...], ck_ref[...], sk_ref[...], p_ref[...]).astype(BF16)
        vone_ref[:, :DIFF_V] = v_ref[...]
        vone_ref[:, DIFF_V:] = jnp.ones((v_ref.shape[0], DIFF_V), BF16)

    lv = lam_ref[...]
    lam = (jnp.exp(jnp.sum(lv[0:1] * lv[1:2], axis=-1, keepdims=True))
           - jnp.exp(jnp.sum(lv[2:3] * lv[3:4], axis=-1, keepdims=True)) + lam_init)

    def scores(qm):
        return lax.dot_general(qm.astype(BF16), kro_ref[...], _NT, preferred_element_type=F32)

    chains = []
    for r, q_ref in enumerate(q_refs):
        rows = slice(r * tq, (r + 1) * tq)
        q = _rope(q_ref[...], cq_ref[rows, :], sq_ref[rows, :], p_ref[...]) * scale
        lane = lax.broadcasted_iota(jnp.int32, q.shape, 1)
        chains += [scores(jnp.where(lane < DIFF_QK, q, 0.0)), scores(jnp.where(lane >= DIFF_QK, q, 0.0))]
    outs = _softmax_values(chains, [vone_ref] * len(chains))
    for r in range(tiles):
        o = outs[2 * r] - lam * outs[2 * r + 1]
        norm = lax.rsqrt(jnp.mean(o * o, axis=-1, keepdims=True) + EPS)
        o_ref[r * tq:(r + 1) * tq, :] = (o * norm * g_ref[...] * (1.0 - lam_init)).astype(o_ref.dtype)


def _diff_attention(hin, tabs, lam_vecs, g_subln, lam_init, n_ctx):
    b, t, _ = hin.shape
    tq = n_ctx
    n = t - n_ctx
    cos, sin, p = tabs
    h_all = DIFF_HEADS
    tiles = DIFF_TILES_PER_STEP
    off = n_ctx // tq
    q_specs = [pl.BlockSpec((None, tq, LANES), functools.partial(lambda bi, h, i, r: (bi, tiles * i + off + r, h), r=r))
               for r in range(tiles)]
    q_tab = pl.BlockSpec((tiles * tq, LANES), lambda bi, h, i: (i, 0))
    return pl.pallas_call(
        functools.partial(_diff_kernel, tiles=tiles, scale=DIFF_QK ** -0.5 * LOG2E, lam_init=lam_init),
        out_shape=jax.ShapeDtypeStruct((b, n, h_all * DIFF_V), BF16),
        grid=(b, h_all, n // (tiles * tq)),
        in_specs=q_specs + [
            pl.BlockSpec((None, t, LANES), lambda bi, h, i: (bi, 0, h_all + h)),
            pl.BlockSpec((None, t, LANES), lambda bi, h, i: (bi, 0, 2 * h_all + h)),
            q_tab, q_tab,
            _const_spec((t, LANES)), _const_spec((t, LANES)),
            _const_spec((LANES, LANES)), _const_spec((4, DIFF_QK)), _const_spec((1, DIFF_V))],
        out_specs=pl.BlockSpec((None, tiles * tq, DIFF_V), lambda bi, h, i: (bi, i, h)),
        scratch_shapes=[pltpu.VMEM((t, LANES), BF16), pltpu.VMEM((t, 2 * DIFF_V), BF16)],
        compiler_params=_params("parallel", "parallel", "arbitrary"),
    )(*([hin] * tiles), hin, hin, cos[n_ctx:], sin[n_ctx:], cos, sin, p, lam_vecs,
      g_subln.reshape(1, DIFF_V).astype(F32))


def _s5_kernel(uf_ref, ub_ref, wb_ref, wc_ref, are_ref, aim_ref, yf_ref, yb_ref, s_ref, hs_ref, h_ref,
               *, tc):
    half = S5_GROUPS // S5_BLOCKS * S5_STATE
    gw = S5_GROUPS // S5_BLOCKS * S5_GROUP
    nl = half // LANES
    u_refs = (uf_ref, ub_ref)
    y_refs = (yf_ref, yb_ref)

    @pl.when(pl.program_id(1) == 0)
    def _():
        h_ref[...] = jnp.zeros_like(h_ref)

    for d in range(2):
        for j in range(S5_BLOCKS):
            bu = jnp.dot(u_refs[d][:, j * gw:(j + 1) * gw], wb_ref[d, j], preferred_element_type=F32)
            for c in range(2 * nl):
                s_ref.at[d, c][pl.ds(j, tc, stride=S5_BLOCKS), :] = bu[:, c * LANES:(c + 1) * LANES]

    a_re = [[are_ref[d, :, c * LANES:(c + 1) * LANES] for c in range(nl)] for d in range(2)]
    a_im = [[aim_ref[d, :, c * LANES:(c + 1) * LANES] for c in range(nl)] for d in range(2)]

    def step(t, h):
        rows = (t * S5_BLOCKS, (tc - 1 - t) * S5_BLOCKS)
        out = []
        for d in range(2):
            row = pl.multiple_of(rows[d], S5_BLOCKS)
            new = [None] * (2 * nl)
            for c in range(nl):
                h_re, h_im = h[d * 2 * nl + c], h[d * 2 * nl + nl + c]
                new[c] = a_re[d][c] * h_re - a_im[d][c] * h_im + s_ref[d, c, pl.ds(row, S5_BLOCKS), :]
                new[nl + c] = (a_re[d][c] * h_im + a_im[d][c] * h_re
                               + s_ref[d, nl + c, pl.ds(row, S5_BLOCKS), :])
            for c in range(2 * nl):
                hs_ref[d, c, pl.ds(row, S5_BLOCKS), :] = new[c]
            out += new
        return tuple(out)

    init = tuple(h_ref[d, c] for d in range(2) for c in range(2 * nl))
    h = lax.fori_loop(0, tc, step, init, unroll=4)
    for d in range(2):
        for c in range(2 * nl):
            h_ref[d, c] = h[d * 2 * nl + c]

    for d in range(2):
        for j in range(S5_BLOCKS):
            hj = jnp.concatenate([hs_ref.at[d, c][pl.ds(j, tc, stride=S5_BLOCKS), :] for c in range(2 * nl)],
                                 axis=1).astype(BF16)
            y_refs[d][:, j * gw:(j + 1) * gw] = jnp.dot(hj, wc_ref[d, j], preferred_element_type=F32)


def _s5_scan(hin, u_block, wb, wc, a_re, a_im, tc):
    b, t, _ = hin.shape
    width = S5_GROUPS * S5_GROUP
    half = S5_GROUPS // S5_BLOCKS * S5_STATE
    nchunk = t // tc

    def back(k):
        return jnp.where(k == 0, 0, nchunk - k)

    full = lambda a: pl.BlockSpec(a.shape, lambda bi, k: (0,) * a.ndim)
    out = jax.ShapeDtypeStruct((b, t, width), F32)
    return pl.pallas_call(
        functools.partial(_s5_kernel, tc=tc),
        out_shape=(out, out),
        grid=(b, nchunk),
        in_specs=[pl.BlockSpec((None, tc, width), lambda bi, k: (bi, k, u_block)),
                  pl.BlockSpec((None, tc, width), lambda bi, k: (bi, back(k), u_block)),
                  full(wb), full(wc), full(a_re), full(a_im)],
        out_specs=(pl.BlockSpec((None, tc, width), lambda bi, k: (bi, k, 0)),
                   pl.BlockSpec((None, tc, width), lambda bi, k: (bi, back(k), 0))),
        scratch_shapes=[pltpu.VMEM((2, 2 * half // LANES, tc * S5_BLOCKS, LANES), F32),
                        pltpu.VMEM((2, 2 * half // LANES, tc * S5_BLOCKS, LANES), F32),
                        pltpu.VMEM((2, 2 * half // LANES, S5_BLOCKS, LANES), F32)],
        compiler_params=_params("parallel", "arbitrary"),
    )(hin, hin, wb, wc, a_re, a_im)


def _s5_weights(lam_re, lam_im, log_dt, b_re, b_im, c_re, c_im):
    lr = lam_re.astype(F32)
    li = lam_im.astype(F32)
    dt = jnp.exp(log_dt.astype(F32))[..., None]
    mag = jnp.exp(lr * dt)
    ab_re = mag * jnp.cos(li * dt)
    ab_im = mag * jnp.sin(li * dt)
    den = lr * lr + li * li
    num_re = ab_re - 1.0
    coef_re = ((num_re * lr + ab_im * li) / den)[..., None]
    coef_im = ((ab_im * lr - num_re * li) / den)[..., None]
    br = b_re.astype(F32)
    bi = b_im.astype(F32)
    bb_re = coef_re * br - coef_im * bi
    bb_im = coef_re * bi + coef_im * br
    nb = S5_BLOCKS
    gb = S5_GROUPS // nb
    eye = jnp.eye(gb, dtype=F32)

    def drive(bb):
        w = jnp.einsum('djgpi,gh->djgihp', bb.reshape(2, nb, gb, S5_STATE, S5_GROUP), eye)
        return w.reshape(2, nb, gb * S5_GROUP, gb * S5_STATE)

    def read(c):
        w = jnp.einsum('djgip,gh->djgphi', c.astype(F32).reshape(2, nb, gb, S5_GROUP, S5_STATE), eye)
        return w.reshape(2, nb, gb * S5_STATE, gb * S5_GROUP)

    wb = jnp.concatenate([drive(bb_re), drive(bb_im)], axis=-1).astype(BF16)
    wc = jnp.concatenate([read(c_re), -read(c_im)], axis=-2).astype(BF16)
    return wb, wc, ab_re.reshape(2, nb, gb * S5_STATE), ab_im.reshape(2, nb, gb * S5_STATE)


def _s5_glu_kernel(yf_ref, yb_ref, u_ref, d_ref, w_ref, o_ref):
    y = yf_ref[...] + yb_ref[...] + d_ref[...] * u_ref[...].astype(F32)
    r = jnp.dot(jax.nn.gelu(y).astype(BF16), w_ref[...], preferred_element_type=F32)
    n = r.shape[1] // 2
    o_ref[...] = (r[:, :n] * jax.nn.sigmoid(r[:, n:])).astype(o_ref.dtype)


def _s5_glu(yf, yb, hin, u_block, d_skip, w_glu, tm):
    b, t, width = yf.shape
    return pl.pallas_call(
        _s5_glu_kernel,
        out_shape=jax.ShapeDtypeStruct((b, t, width), BF16),
        grid=(b, t // tm),
        in_specs=[pl.BlockSpec((None, tm, width), lambda bi, i: (bi, i, 0)),
                  pl.BlockSpec((None, tm, width), lambda bi, i: (bi, i, 0)),
                  pl.BlockSpec((None, tm, width), lambda bi, i: (bi, i, u_block)),
                  pl.BlockSpec((1, width), lambda bi, i: (0, 0)),
                  pl.BlockSpec(w_glu.shape, lambda bi, i: (0, 0))],
        out_specs=pl.BlockSpec((None, tm, width), lambda bi, i: (bi, _rotated_tile(i, t // tm), 0)),
        compiler_params=_params("parallel", "parallel"),
    )(yf, yb, hin, d_skip.reshape(1, width).astype(F32), w_glu)


CONV_HALO = 16


def _conv_kernel(ap_ref, gp_ref, ac_ref, gc_ref, an_ref, gn_ref, w_ref, b_ref, lg_ref, lb_ref,
                 o_ref, v_ref, sh_ref, y_ref, *, tr, taps):
    i = pl.program_id(1)
    last = pl.num_programs(1) - 1

    def glu(a_ref, g_ref):
        return a_ref[...].astype(F32) * jax.nn.sigmoid(g_ref[...].astype(F32))

    v_ref[0:CONV_HALO, :] = jnp.where(i > 0, glu(ap_ref, gp_ref), 0.0)
    v_ref[CONV_HALO:CONV_HALO + tr, :] = glu(ac_ref, gc_ref)
    v_ref[CONV_HALO + tr:, :] = jnp.where(i < last, glu(an_ref, gn_ref), 0.0)

    span = v_ref.shape[0] - SUBLANES
    for r in range(1, SUBLANES):
        sh_ref[r - 1, 0:span, :] = v_ref[r:r + span, :]

    first = CONV_HALO - taps // 2
    for c0 in range(0, v_ref.shape[1], LANES):
        acc = jnp.zeros((tr, LANES), F32)
        for k in range(taps):
            tile, r = divmod(first + k, SUBLANES)
            src = v_ref if r == 0 else sh_ref.at[r - 1]
            window = src[tile * SUBLANES:tile * SUBLANES + tr, c0:c0 + LANES]
            acc = acc + w_ref[k:k + 1, c0:c0 + LANES] * window
        y_ref[:, c0:c0 + LANES] = acc

    y = y_ref[...] + b_ref[...]
    yc = y - jnp.mean(y, axis=-1, keepdims=True)
    yn = yc * lax.rsqrt(jnp.mean(yc * yc, axis=-1, keepdims=True) + EPS) * lg_ref[...] + lb_ref[...]
    o_ref[...] = (yn * jax.nn.sigmoid(yn)).astype(o_ref.dtype)


def _conformer_conv(hin, a_block, w_dw, b_dw, g_n, b_n, n_ctx, tr):
    b, t, _ = hin.shape
    n = t - n_ctx
    taps, ch = w_dw.shape
    hb = tr // CONV_HALO
    off = n_ctx // tr
    last_halo = t // CONV_HALO - 1

    def prev_map(cb):
        return lambda bi, i: (bi, (i + off) * hb - 1, cb)

    def next_map(cb):
        return lambda bi, i: (bi, jnp.minimum((i + off + 1) * hb, last_halo), cb)

    def cur_map(cb):
        return lambda bi, i: (bi, i + off, cb)

    vec = lambda v: v.reshape(1, ch).astype(F32)
    halo = lambda m: pl.BlockSpec((None, CONV_HALO, ch), m)
    full = lambda shape: pl.BlockSpec(shape, lambda bi, i: (0, 0))
    return pl.pallas_call(
        functools.partial(_conv_kernel, tr=tr, taps=taps),
        out_shape=jax.ShapeDtypeStruct((b, n, ch), BF16),
        grid=(b, n // tr),
        in_specs=[halo(prev_map(a_block)), halo(prev_map(a_block + 1)),
                  pl.BlockSpec((None, tr, ch), cur_map(a_block)),
                  pl.BlockSpec((None, tr, ch), cur_map(a_block + 1)),
                  halo(next_map(a_block)), halo(next_map(a_block + 1)),
                  full((taps, ch)), full((1, ch)), full((1, ch)), full((1, ch))],
        out_specs=pl.BlockSpec((None, tr, ch), lambda bi, i: (bi, i, 0)),
        scratch_shapes=[pltpu.VMEM((tr + 2 * CONV_HALO, ch), F32),
                        pltpu.VMEM((SUBLANES - 1, tr + 2 * CONV_HALO, ch), F32),
                        pltpu.VMEM((tr, ch), F32)],
        compiler_params=_params("parallel", "parallel"),
    )(hin, hin, hin, hin, hin, hin, w_dw.astype(F32), vec(b_dw), vec(g_n), vec(b_n))


def _ffn_matmuls(h, w1, w3, w2, layer, mod=None):
    b, rows, d = h.shape
    outs = _swiglu(h.reshape(b * rows, d), w1, w3, w2, layer, _row_tile(b * rows, ROW_TILE_CAP), FF_COL_TILE, mod)
    act, w2b = outs[:2]
    y = _matmul([(act, 0, w2b, 0, w2b.shape[0])], _row_tile(b * rows, LONG_K_ROW_TILE_CAP), DOWN_COL_TILE)
    return y.reshape(b, rows, d), (outs[2] if mod is not None else None)


def kernel(x, c, ctx, c_ctx, w_mod, b_mod, g_mix_pre, g_mix_post, g_ffn_pre, g_ffn_post, w_ff1, w_ff3, w_ff2, e_w_in, e_g_q, e_w_uq, e_g_kv, e_w_ukv, e_s5_lam_re, e_s5_lam_im, e_s5_log_dt, e_s5_b_re, e_s5_b_im, e_s5_c_re, e_s5_c_im, e_s5_d, e_w_glu, e_w_out, o_w_in, o_lam_q1, o_lam_k1, o_lam_q2, o_lam_k2, o_g_subln, o_conv_w, o_conv_b, o_conv_norm_g, o_conv_norm_b, o_w_out):
    b, n, d = x.shape
    n_ctx = ctx.shape[1]
    t = n_ctx + n
    depth = w_mod.shape[0]
    assert depth == 2 and e_w_in.shape[0] == 1 and o_w_in.shape[0] == 1
    assert b + 1 <= SUBLANES and n % n_ctx == 0
    tr = n_ctx
    q_rank = e_g_q.shape[1]
    kv_rank = e_g_kv.shape[1]
    s5_w = S5_GROUPS * S5_GROUP

    cvec = jnp.zeros((SUBLANES, d), F32).at[:b].set(c).at[b].set(c_ctx)
    b_mod3 = b_mod.reshape(depth, 1, N_MOD * d)

    def modtab(rows):
        lat = rows[:b].reshape(b, 1, N_MOD, d)
        cx = jnp.broadcast_to(rows[b].reshape(1, 1, N_MOD, d), (b, 1, N_MOD, d))
        return jnp.concatenate([cx, lat], axis=1)

    modtabs = [modtab(_modulation(cvec, w_mod, b_mod3, 1)[0]), None]

    cos, sin = _rope_tables(n, MLA_ROPE)
    rot = _rot_matrix(MLA_ROPE)
    hp = MLA_HEAD_PAD
    cos2 = np.concatenate([cos, cos], axis=1)
    sin2 = np.concatenate([sin, sin], axis=1)
    rot2 = _embed(rot, (LANES, LANES), 0, 0, 0.0)
    rot2[DIFF_QK:, DIFF_QK:] = rot
    pair_tabs = (_embed(cos2, (t, LANES), n_ctx, 0, 1.0), _embed(sin2, (t, LANES), n_ctx, 0, 0.0),
                 jnp.asarray(rot2, BF16))
    q_tab = _embed(np.concatenate([cos, sin], axis=1), (t, hp), n_ctx, MLA_NOPE, 1.0)
    q_tab[:n_ctx, MLA_NOPE + MLA_ROPE:] = 0.0

    lw = e_w_in[0]
    cuts = [q_rank, q_rank + kv_rank, q_rank + kv_rank + MLA_ROPE]
    w_in = jnp.concatenate([lw[:, :cuts[0]], lw[:, cuts[2]:], lw[:, cuts[0]:cuts[1]], lw[:, cuts[1]:cuts[2]],
                            lw[:, cuts[1]:cuts[2]]], axis=1).astype(BF16)
    u_block = q_rank // s5_w
    kv_block = (q_rank + s5_w) // kv_rank
    kr_block = (q_rank + s5_w + kv_rank) // LANES
    w_uq = e_w_uq[0].reshape(q_rank, MLA_HEADS, MLA_NOPE + MLA_ROPE)
    w_uq = jnp.concatenate([w_uq, jnp.einsum('khj,ji->khi', w_uq[:, :, MLA_NOPE:], rot)], axis=2)
    w_uq = w_uq.reshape(q_rank, MLA_HEADS * hp).astype(BF16)
    n_attn = MLA_HEADS * MLA_V

    h = _norm_mod((ctx, x), g_mix_pre[0], modtabs[0], SH_M, SC_M, tr).reshape(b * t, d)
    tm = _row_tile(b * t, ROW_TILE_CAP)
    hin = _matmul([(h, 0, w_in, 0, d)], tm, w_in.shape[1] // 3)
    q = _matmul([(hin, 0, w_uq, 0, q_rank)], tm, UP_COL_TILE, gain=e_g_q[0])
    kv = _matmul([(hin, kv_block, e_w_ukv, 0, kv_rank)], tm, UP_COL_TILE, gain=e_g_kv[0])
    hin3 = hin.reshape(b, t, -1)
    attn = _mla_attention(q.reshape(b, t, -1), kv.reshape(b, t, -1), hin3, kr_block,
                          (q_tab,) + pair_tabs, n_ctx)
    wb, wc, a_re, a_im = _s5_weights(e_s5_lam_re[0], e_s5_lam_im[0], e_s5_log_dt[0], e_s5_b_re[0],
                                     e_s5_b_im[0], e_s5_c_re[0], e_s5_c_im[0])
    yf, yb = _s5_scan(hin3, u_block, wb, wc, a_re, a_im, tr)
    s5 = _s5_glu(yf, yb, hin3, u_block, e_s5_d[0], e_w_glu[0].astype(BF16), tr)
    y = _matmul([(attn.reshape(b * t, -1), 0, e_w_out, 0, n_attn),
                 (s5.reshape(b * t, -1), 0, e_w_out, n_attn // s5_w, s5_w)], tm, PROJ_COL_TILE)
    z, h = _residual((ctx, x), y.reshape(b, t, d), g_mix_post[0], modtabs[0], GT_M, 0, 0, tr,
                     nxt=(g_ffn_pre[0], modtabs[0], SH_F, SC_F), y_rotated=True)
    y, mod1 = _ffn_matmuls(h, w_ff1, w_ff3, w_ff2, 0, mod=(cvec, w_mod, b_mod3, 1))
    modtabs[1] = modtab(mod1)
    z, h = _residual(z, y, g_ffn_post[0], modtabs[0], GT_F, 0, 0, tr,
                     nxt=(g_mix_pre[1], modtabs[1], SH_M, SC_M))

    lam_init = 0.8 - 0.6 * math.exp(-0.3 * 1)
    n_attn = DIFF_HEADS * DIFF_V
    conv_ch = o_conv_w.shape[2]

    hin3 = _matmul([(h.reshape(b * t, d), 0, o_w_in, 0, d)], tm, PROJ_COL_TILE).reshape(b, t, -1)
    lam_vecs = jnp.stack([o_lam_q1[0], o_lam_k1[0], o_lam_q2[0], o_lam_k2[0]]).astype(F32)
    attn = _diff_attention(hin3, pair_tabs, lam_vecs, o_g_subln[0], lam_init, n_ctx)
    conv = _conformer_conv(hin3, 3 * n_attn // conv_ch, o_conv_w[0], o_conv_b[0], o_conv_norm_g[0],
                           o_conv_norm_b[0], n_ctx, tr)
    tml = _row_tile(b * n, ROW_TILE_CAP)
    y = _matmul([(attn.reshape(b * n, -1), 0, o_w_out, 0, n_attn),
                 (conv.reshape(b * n, -1), 0, o_w_out, n_attn // conv_ch, conv_ch)], tml, PROJ_COL_TILE)
    xl, h = _residual(z, y.reshape(b, n, d), g_mix_post[1], modtabs[1], GT_M, n_ctx // tr, 1, tr,
                      nxt=(g_ffn_pre[1], modtabs[1], SH_F, SC_F))
    y, _ = _ffn_matmuls(h, w_ff1, w_ff3, w_ff2, 1)
    return _residual(xl, y, g_ffn_post[1], modtabs[1], GT_F, 0, 1, tr)
```

```python
import functools
import math

import numpy as np

import jax
import jax.numpy as jnp
from jax import lax
from jax.experimental import pallas as pl
from jax.experimental.pallas import tpu as pltpu

F32 = jnp.float32
BF16 = jnp.bfloat16

EPS = 1e-6
ROPE_BASE = 10000.0
GRID_W = 64
LOG2E = math.log2(math.e)

MLA_HEADS = 16
MLA_NOPE = 128
MLA_ROPE = 64
MLA_V = 128
MLA_HEAD_PAD = 256
MLA_HEADS_PER_STEP = 2
DIFF_HEADS = 16
DIFF_QK = 64
DIFF_V = 128
S5_GROUP = 16
S5_GROUPS = 64
S5_STATE = 64
S5_BLOCKS = 8
N_MOD = 6

LANES = 128
SUBLANES = 8
VMEM_LIMIT = 56 * 1024 * 1024

ROW_TILE_CAP = 1100
LONG_K_ROW_TILE_CAP = 550
FF_COL_TILE = 256
DOWN_COL_TILE = 512
PROJ_COL_TILE = 512
UP_COL_TILE = 2048
MOD_COL_TILE = 512

SH_M, SC_M, GT_M, SH_F, SC_F, GT_F = range(N_MOD)


def _params(*sem):
    return pltpu.CompilerParams(dimension_semantics=sem, vmem_limit_bytes=VMEM_LIMIT)


def _row_tile(rows, cap):
    k = -(-rows // cap)
    while rows % k or (rows // k) % 16:
        k += 1
    return rows // k


def _mod_rows(c_ref, w_ref, b_ref):
    c = c_ref[...]
    a = (c * jax.nn.sigmoid(c)).astype(BF16)
    return jnp.dot(a, w_ref[...].astype(BF16), preferred_element_type=F32) + b_ref[...]


def _mod_kernel(c_ref, w_ref, b_ref, o_ref):
    o_ref[...] = _mod_rows(c_ref, w_ref, b_ref)


def _modulation(cvec, w_mod, b_mod, layers):
    _, d, n = w_mod.shape
    tn = MOD_COL_TILE
    return pl.pallas_call(
        _mod_kernel,
        out_shape=jax.ShapeDtypeStruct((layers, SUBLANES, n), F32),
        grid=(layers, n // tn),
        in_specs=[pl.BlockSpec((SUBLANES, d), lambda l, j: (0, 0)),
                  pl.BlockSpec((None, d, tn), lambda l, j: (l, 0, j)),
                  pl.BlockSpec((None, 1, tn), lambda l, j: (l, 0, j))],
        out_specs=pl.BlockSpec((None, SUBLANES, tn), lambda l, j: (l, 0, j)),
        compiler_params=_params("parallel", "parallel"),
    )(cvec, w_mod, b_mod)


def _stream_specs(src, tr, d, tile_offset):
    if isinstance(src, tuple):
        return ([pl.BlockSpec((None, tr, d), lambda bi, i: (bi, 0, 0)),
                 pl.BlockSpec((None, tr, d), lambda bi, i: (bi, jnp.maximum(i - 1, 0), 0))], list(src))
    return [pl.BlockSpec((None, tr, d), lambda bi, i: (bi, i + tile_offset, 0))], [src]


def _stream_rows(x_refs):
    if len(x_refs) == 1:
        return x_refs[0][...]
    return jnp.where(pl.program_id(1) == 0, x_refs[0][...], x_refs[1][...])


def _mod_spec(d, seg_offset):
    return pl.BlockSpec((None, None, N_MOD, d), lambda bi, i: (bi, jnp.minimum(i + seg_offset, 1), 0, 0))


def _adaln(x, g_ref, m_ref, shift_row, scale_row):
    r = lax.rsqrt(jnp.mean(x * x, axis=-1, keepdims=True) + EPS)
    h = x * r * g_ref[...]
    return (h * (1.0 + m_ref[scale_row:scale_row + 1, :]) + m_ref[shift_row:shift_row + 1, :]).astype(BF16)


def _norm_mod_kernel(*refs, n_src, shift_row, scale_row):
    g_ref, m_ref, o_ref = refs[n_src:]
    o_ref[...] = _adaln(_stream_rows(refs[:n_src]), g_ref, m_ref, shift_row, scale_row)


def _norm_mod(src, gain, modtab, shift_row, scale_row, tr):
    specs, args = _stream_specs(src, tr, gain.shape[0], 0)
    b = args[0].shape[0]
    d = gain.shape[0]
    rows = sum(a.shape[1] for a in args)
    return pl.pallas_call(
        functools.partial(_norm_mod_kernel, n_src=len(args), shift_row=shift_row, scale_row=scale_row),
        out_shape=jax.ShapeDtypeStruct((b, rows, d), BF16),
        grid=(b, rows // tr),
        in_specs=specs + [pl.BlockSpec((1, d), lambda bi, i: (0, 0)), _mod_spec(d, 0)],
        out_specs=pl.BlockSpec((None, tr, d), lambda bi, i: (bi, i, 0)),
        compiler_params=_params("parallel", "parallel"),
    )(*args, gain.reshape(1, d), modtab)


def _residual_kernel(*refs, n_src, gate_row, nxt_rows):
    y_ref, g_ref, m_ref = refs[n_src:n_src + 3]
    y = y_ref[...].astype(F32)
    r = lax.rsqrt(jnp.mean(y * y, axis=-1, keepdims=True) + EPS)
    z = _stream_rows(refs[:n_src]) + m_ref[gate_row:gate_row + 1, :] * (y * r * g_ref[...])
    if nxt_rows is None:
        refs[-1][...] = z
    else:
        g2_ref, m2_ref, o_ref, h_ref = refs[n_src + 3:]
        o_ref[...] = z
        h_ref[...] = _adaln(z, g2_ref, m2_ref, *nxt_rows)


def _residual(src, y, gain, modtab, gate_row, x_tile_offset, seg_offset, tr, nxt=None, y_rotated=False):
    b, rows, d = y.shape
    specs, args = _stream_specs(src, tr, d, x_tile_offset)
    tile = lambda: pl.BlockSpec((None, tr, d), lambda bi, i: (bi, i, 0))
    vec = lambda: pl.BlockSpec((1, d), lambda bi, i: (0, 0))
    y_spec = tile()
    if y_rotated:
        y_spec = pl.BlockSpec((None, tr, d), lambda bi, i: (bi, _rotated_tile(i, rows // tr), 0))
    in_specs = specs + [y_spec, vec(), _mod_spec(d, seg_offset)]
    args = args + [y, gain.reshape(1, d), modtab]
    out_shape = jax.ShapeDtypeStruct((b, rows, d), F32)
    out_specs = tile()
    nxt_rows = None
    if nxt is not None:
        in_specs += [vec(), _mod_spec(d, seg_offset)]
        args += [nxt[0].reshape(1, d), nxt[1]]
        nxt_rows = (nxt[2], nxt[3])
        out_shape = (out_shape, jax.ShapeDtypeStruct((b, rows, d), BF16))
        out_specs = (out_specs, tile())
    return pl.pallas_call(
        functools.partial(_residual_kernel, n_src=len(specs), gate_row=gate_row, nxt_rows=nxt_rows),
        out_shape=out_shape,
        grid=(b, rows // tr),
        in_specs=in_specs,
        out_specs=out_specs,
        compiler_params=_params("parallel", "parallel"),
    )(*args)


def _bf16(w):
    return w if w.dtype == BF16 else w.astype(BF16)


def _matmul_kernel(*refs, n_pairs, normed):
    o_ref = refs[-1]
    acc = None
    for p in range(n_pairs):
        a = refs[2 * p][...]
        if normed:
            af = a.astype(F32)
            r = lax.rsqrt(jnp.mean(af * af, axis=-1, keepdims=True) + EPS)
            a = (af * r * refs[2 * n_pairs][...]).astype(BF16)
        part = jnp.dot(a, _bf16(refs[2 * p + 1][...]), preferred_element_type=F32)
        acc = part if acc is None else acc + part
    o_ref[...] = acc.astype(o_ref.dtype)


def _weight_spec(w, k, row_block, layer, tn):
    if w.ndim == 3:
        return pl.BlockSpec((None, k, tn), lambda i, j: (layer, row_block, j))
    return pl.BlockSpec((k, tn), lambda i, j: (row_block, j))


def _matmul(pairs, tm, tn, gain=None, layer=0, out_dtype=BF16):
    rows = pairs[0][0].shape[0]
    n = pairs[0][2].shape[-1]
    tn = min(tn, n)
    in_specs, args = [], []
    for a, cb, w, rb, k in pairs:
        in_specs.append(pl.BlockSpec((tm, k), functools.partial(lambda i, j, cb: (i, cb), cb=cb)))
        in_specs.append(_weight_spec(w, k, rb, layer, tn))
        args += [a, w]
    if gain is not None:
        k = pairs[0][4]
        in_specs.append(pl.BlockSpec((1, k), lambda i, j: (0, 0)))
        args.append(gain.reshape(1, k).astype(F32))
    return pl.pallas_call(
        functools.partial(_matmul_kernel, n_pairs=len(pairs), normed=gain is not None),
        out_shape=jax.ShapeDtypeStruct((rows, n), out_dtype),
        grid=(rows // tm, n // tn),
        in_specs=in_specs,
        out_specs=pl.BlockSpec((tm, tn), lambda i, j: (i, j)),
        compiler_params=_params("parallel", "arbitrary"),
    )(*args)


def _swiglu_kernel(*refs, mod_tiles):
    a_ref, w1_ref, w3_ref, w2_ref = refs[:4]
    o_ref, w2b_ref = refs[-2 - (mod_tiles > 0):][:2]
    a = a_ref[...]
    g = jnp.dot(a, _bf16(w1_ref[...]), preferred_element_type=F32)
    u = jnp.dot(a, _bf16(w3_ref[...]), preferred_element_type=F32)
    o_ref[...] = (g * jax.nn.sigmoid(g) * u).astype(o_ref.dtype)

    @pl.when(pl.program_id(0) == 0)
    def _():
        w2b_ref[...] = w2_ref[...].astype(BF16)

    if mod_tiles:
        c_ref, wm_ref, bm_ref = refs[4:7]

        @pl.when(pl.program_id(0) * pl.num_programs(1) + pl.program_id(1) < mod_tiles)
        def _():
            refs[-1][...] = _mod_rows(c_ref, wm_ref, bm_ref)


def _swiglu(a, w1, w3, w2, layer, tm, tn, mod=None):
    rows, k = a.shape
    f = w1.shape[2]
    d = w2.shape[2]
    ni, nj = rows // tm, f // tn

    def slab(i, j):
        return jnp.where(i == 0, j, nj - 1)

    in_specs = [pl.BlockSpec((tm, k), lambda i, j: (i, 0)),
                _weight_spec(w1, k, 0, layer, tn), _weight_spec(w3, k, 0, layer, tn),
                pl.BlockSpec((None, tn, d), lambda i, j: (layer, slab(i, j), 0))]
    out_shape = [jax.ShapeDtypeStruct((rows, f), BF16), jax.ShapeDtypeStruct((f, d), BF16)]
    out_specs = [pl.BlockSpec((tm, tn), lambda i, j: (i, j)),
                 pl.BlockSpec((tn, d), lambda i, j: (slab(i, j), 0))]
    args = [a, w1, w3, w2]
    mod_tiles = 0
    if mod is not None:
        cvec, w_mod, b_mod, ml = mod
        n_mod = w_mod.shape[2]
        per_step = -(-n_mod // (LANES * ni * nj))
        while (n_mod // LANES) % per_step:
            per_step += 1
        mod_tile = per_step * LANES
        mod_tiles = n_mod // mod_tile

        def tile(i, j):
            return jnp.minimum(i * nj + j, mod_tiles - 1)

        in_specs += [pl.BlockSpec(cvec.shape, lambda i, j: (0, 0)),
                     pl.BlockSpec((None, w_mod.shape[1], mod_tile), lambda i, j: (ml, 0, tile(i, j))),
                     pl.BlockSpec((None, 1, mod_tile), lambda i, j: (ml, 0, tile(i, j)))]
        out_shape.append(jax.ShapeDtypeStruct((SUBLANES, n_mod), F32))
        out_specs.append(pl.BlockSpec((SUBLANES, mod_tile), lambda i, j: (0, tile(i, j))))
        args += [cvec, w_mod, b_mod]
    return pl.pallas_call(
        functools.partial(_swiglu_kernel, mod_tiles=mod_tiles),
        out_shape=tuple(out_shape),
        grid=(ni, nj),
        in_specs=in_specs,
        out_specs=tuple(out_specs),
        compiler_params=_params("arbitrary", "arbitrary"),
    )(*args)


def _rope_tables(n_tok, dim):
    rows = n_tok // GRID_W
    row = np.repeat(np.arange(rows, dtype=np.float32), GRID_W)
    col = np.tile(np.arange(GRID_W, dtype=np.float32), rows)
    axis_dim = dim // 2
    expo = -np.arange(0, axis_dim, 2, dtype=np.float32) / np.float32(axis_dim)
    inv = np.power(np.float32(ROPE_BASE), expo).astype(np.float32)
    ang_r = row[:, None] * inv[None, :]
    ang_c = col[:, None] * inv[None, :]
    ang = np.concatenate([ang_r, ang_r, ang_c, ang_c], axis=-1).astype(np.float32)
    return np.cos(ang), np.sin(ang)


def _rot_matrix(dim):
    q = dim // 4
    p = np.zeros((dim, dim), np.float32)
    eye = np.eye(q, dtype=np.float32)
    p[q:2 * q, 0:q] = -eye
    p[0:q, q:2 * q] = eye
    p[3 * q:, 2 * q:3 * q] = -eye
    p[2 * q:3 * q, 3 * q:] = eye
    return p


def _signed(sin, quarter):
    lane = np.arange(sin.shape[1])
    return np.where(lane % (2 * quarter) < quarter, -sin, sin).astype(np.float32)


def _embed(block, shape, r0, c0, fill):
    out = np.full(shape, fill, np.float32)
    out[r0:r0 + block.shape[0], c0:c0 + block.shape[1]] = block
    return out


def _rope(x_bf16, cos, sin, rot):
    turned = jnp.dot(x_bf16, rot, preferred_element_type=F32)
    return x_bf16.astype(F32) * cos + turned * sin


def _rope_lanes(x_bf16, cos, sin_signed, quarter):
    x = x_bf16.astype(F32)
    lane = lax.broadcasted_iota(jnp.int32, x.shape, 1)
    right = pltpu.roll(x, x.shape[1] - quarter, 1)
    left = pltpu.roll(x, quarter, 1)
    turned = jnp.where(lane % (2 * quarter) < quarter, right, left)
    return x * cos + turned * sin_signed


KEY_CHUNK = 1024


def _softmax_values(scores, vones):
    keys = scores[0].shape[1]
    maxes = [jnp.max(s, axis=-1, keepdims=True) for s in scores]
    accs = [None] * len(scores)
    for c0 in range(0, keys, KEY_CHUNK):
        c1 = min(c0 + KEY_CHUNK, keys)
        for i, (s, m, vone) in enumerate(zip(scores, maxes, vones)):
            e = jnp.exp2(s[:, c0:c1] - m).astype(BF16)
            part = jnp.dot(e, vone[c0:c1, :], preferred_element_type=F32)
            accs[i] = part if accs[i] is None else accs[i] + part
    dv = accs[0].shape[1] // 2
    return [a[:, :dv] / a[:, dv:] for a in accs]


def _const_spec(shape):
    return pl.BlockSpec(shape, lambda bi, h, i: (0,) * len(shape), pipeline_mode=pl.Buffered(1))


_NT = (((1,), (1,)), ((), ()))


def _mla_kernel(*refs, tiles, n_ctx, scale, heads):
    q_refs, t_refs = refs[:tiles], refs[tiles:2 * tiles]
    kv_ref, kr_ref, ck_ref, sk_ref, pk_ref, o_ref, kcat_ref, vone_ref = refs[2 * tiles:]
    step = pl.program_id(2)
    hp = MLA_HEAD_PAD
    tq = q_refs[0].shape[0]

    @pl.when(step == 0)
    def _():
        kr = _rope(kr_ref[...], ck_ref[...], sk_ref[...], pk_ref[...]).astype(BF16)
        ones = jnp.ones((kr.shape[0], hp - MLA_V), BF16)
        for h in range(heads):
            kcat_ref[h, :, :MLA_NOPE] = kv_ref[:, h * hp:h * hp + MLA_NOPE]
            kcat_ref[h, :, MLA_NOPE:] = kr
            vone_ref[h, :, :MLA_V] = kv_ref[:, h * hp + MLA_NOPE:(h + 1) * hp]
            vone_ref[h, :, MLA_V:] = ones

    def attend(n_tiles, rows):
        scores = []
        for r in range(n_tiles):
            for h in range(heads):
                q = q_refs[r][:, h * hp:(h + 1) * hp].astype(F32) * t_refs[r][...] * scale
                scores.append(lax.dot_general(q.astype(BF16), kcat_ref[h, :rows, :], _NT,
                                              preferred_element_type=F32))
        outs = _softmax_values(scores, [vone_ref.at[h, :rows, :] for _ in range(n_tiles) for h in range(heads)])
        for r in range(n_tiles):
            for h in range(heads):
                o_ref[r * tq:(r + 1) * tq, h * MLA_V:(h + 1) * MLA_V] = outs[r * heads + h].astype(o_ref.dtype)

    @pl.when(step == 0)
    def _():
        attend(1, n_ctx)

    @pl.when(step > 0)
    def _():
        attend(tiles, kcat_ref.shape[1])


MLA_TILES_PER_STEP = 2


def _mla_attention(q, kv, hin, kr_block, tabs, n_ctx):
    b, t, _ = q.shape
    tq = n_ctx
    tq_tab, ck, sk, pk = tabs
    scale = (MLA_NOPE + MLA_ROPE) ** -0.5 * LOG2E
    hp = MLA_HEAD_PAD
    hs = MLA_HEADS_PER_STEP
    tiles = MLA_TILES_PER_STEP
    n_lat = (t - n_ctx) // tq
    assert hp == 2 * MLA_V and hp == MLA_NOPE + 2 * MLA_ROPE == MLA_NOPE + LANES and n_lat % tiles == 0

    def q_tile(s, r):
        return jnp.where(s == 0, r, tiles * (s - 1) + 1 + r)

    def out_block(s):
        return jnp.where(s == 0, n_lat // tiles, s - 1)

    q_specs = [pl.BlockSpec((None, tq, hs * hp), functools.partial(lambda bi, h, s, r: (bi, q_tile(s, r), h), r=r))
               for r in range(tiles)]
    t_specs = [pl.BlockSpec((tq, hp), functools.partial(lambda bi, h, s, r: (q_tile(s, r), 0), r=r))
               for r in range(tiles)]
    return pl.pallas_call(
        functools.partial(_mla_kernel, tiles=tiles, n_ctx=n_ctx, scale=scale, heads=hs),
        out_shape=jax.ShapeDtypeStruct((b, t, MLA_HEADS * MLA_V), BF16),
        grid=(b, MLA_HEADS // hs, 1 + n_lat // tiles),
        in_specs=q_specs + t_specs + [
            pl.BlockSpec((None, t, hs * hp), lambda bi, h, s: (bi, 0, h)),
            pl.BlockSpec((None, t, LANES), lambda bi, h, s: (bi, 0, kr_block)),
            _const_spec((t, LANES)), _const_spec((t, LANES)), _const_spec((LANES, LANES))],
        out_specs=pl.BlockSpec((None, tiles * tq, hs * MLA_V), lambda bi, h, s: (bi, out_block(s), h)),
        scratch_shapes=[pltpu.VMEM((hs, t, hp), BF16), pltpu.VMEM((hs, t, hp), BF16)],
        compiler_params=_params("parallel", "parallel", "arbitrary"),
    )(*([q] * tiles), *([tq_tab] * tiles), kv, hin, ck, sk, pk)


def _rotated_tile(i, n_tiles):
    return jnp.where(i == 0, n_tiles - 1, i - 1)


DIFF_TILES_PER_STEP = 2


def _diff_kernel(*refs, tiles, scale, lam_init):
    q_refs = refs[:tiles]
    k_ref, v_ref, cq_ref, sq_ref, ck_ref, sk_ref, p_ref, lam_ref, g_ref, o_ref, kro_ref, vone_ref = refs[tiles:]
    tq = q_refs[0].shape[0]

    @pl.when(pl.program_id(2) == 0)
    def _():
        kro_ref[...] = _rope(k_ref[...], ck_ref[...], sk_ref[...], p_ref[...]).astype(BF16)
        vone_ref[:, :DIFF_V] = v_ref[...]
        vone_ref[:, DIFF_V:] = jnp.ones((v_ref.shape[0], DIFF_V), BF16)

    lv = lam_ref[...]
    lam = (jnp.exp(jnp.sum(lv[0:1] * lv[1:2], axis=-1, keepdims=True))
           - jnp.exp(jnp.sum(lv[2:3] * lv[3:4], axis=-1, keepdims=True)) + lam_init)

    def scores(qm):
        return lax.dot_general(qm.astype(BF16), kro_ref[...], _NT, preferred_element_type=F32)

    chains = []
    for r, q_ref in enumerate(q_refs):
        rows = slice(r * tq, (r + 1) * tq)
        q = _rope_lanes(q_ref[...], cq_ref[rows, :], sq_ref[rows, :], DIFF_QK // 4) * scale
        lane = lax.broadcasted_iota(jnp.int32, q.shape, 1)
        chains += [scores(jnp.where(lane < DIFF_QK, q, 0.0)), scores(jnp.where(lane >= DIFF_QK, q, 0.0))]
    outs = _softmax_values(chains, [vone_ref] * len(chains))
    for r in range(tiles):
        o = outs[2 * r] - lam * outs[2 * r + 1]
        norm = lax.rsqrt(jnp.mean(o * o, axis=-1, keepdims=True) + EPS)
        o_ref[r * tq:(r + 1) * tq, :] = (o * norm * g_ref[...] * (1.0 - lam_init)).astype(o_ref.dtype)


def _diff_attention(hin, tabs, lam_vecs, g_subln, lam_init, n_ctx):
    b, t, _ = hin.shape
    tq = n_ctx
    n = t - n_ctx
    cos, sin, p = tabs
    h_all = DIFF_HEADS
    tiles = DIFF_TILES_PER_STEP
    off = n_ctx // tq
    q_specs = [pl.BlockSpec((None, tq, LANES), functools.partial(lambda bi, h, i, r: (bi, tiles * i + off + r, h), r=r))
               for r in range(tiles)]
    q_tab = pl.BlockSpec((tiles * tq, LANES), lambda bi, h, i: (i, 0))
    return pl.pallas_call(
        functools.partial(_diff_kernel, tiles=tiles, scale=DIFF_QK ** -0.5 * LOG2E, lam_init=lam_init),
        out_shape=jax.ShapeDtypeStruct((b, n, h_all * DIFF_V), BF16),
        grid=(b, h_all, n // (tiles * tq)),
        in_specs=q_specs + [
            pl.BlockSpec((None, t, LANES), lambda bi, h, i: (bi, 0, h_all + h)),
            pl.BlockSpec((None, t, LANES), lambda bi, h, i: (bi, 0, 2 * h_all + h)),
            q_tab, q_tab,
            _const_spec((t, LANES)), _const_spec((t, LANES)),
            _const_spec((LANES, LANES)), _const_spec((4, DIFF_QK)), _const_spec((1, DIFF_V))],
        out_specs=pl.BlockSpec((None, tiles * tq, DIFF_V), lambda bi, h, i: (bi, i, h)),
        scratch_shapes=[pltpu.VMEM((t, LANES), BF16), pltpu.VMEM((t, 2 * DIFF_V), BF16)],
        compiler_params=_params("parallel", "parallel", "arbitrary"),
    )(*([hin] * tiles), hin, hin, cos[n_ctx:], _signed(sin[n_ctx:], DIFF_QK // 4), cos, sin, p, lam_vecs,
      g_subln.reshape(1, DIFF_V).astype(F32))


def _s5_kernel(uf_ref, ub_ref, wb_ref, wc_ref, are_ref, aim_ref, yf_ref, yb_ref, s_ref, hs_ref, h_ref,
               *, tc):
    half = S5_GROUPS // S5_BLOCKS * S5_STATE
    gw = S5_GROUPS // S5_BLOCKS * S5_GROUP
    nl = half // LANES
    u_refs = (uf_ref, ub_ref)
    y_refs = (yf_ref, yb_ref)

    @pl.when(pl.program_id(1) == 0)
    def _():
        h_ref[...] = jnp.zeros_like(h_ref)

    for d in range(2):
        for j in range(S5_BLOCKS):
            bu = jnp.dot(u_refs[d][:, j * gw:(j + 1) * gw], wb_ref[d, j], preferred_element_type=F32)
            for c in range(2 * nl):
                s_ref.at[d, c][pl.ds(j, tc, stride=S5_BLOCKS), :] = bu[:, c * LANES:(c + 1) * LANES]

    a_re = [[are_ref[d, :, c * LANES:(c + 1) * LANES] for c in range(nl)] for d in range(2)]
    a_im = [[aim_ref[d, :, c * LANES:(c + 1) * LANES] for c in range(nl)] for d in range(2)]

    def step(t, h):
        rows = (t * S5_BLOCKS, (tc - 1 - t) * S5_BLOCKS)
        out = []
        for d in range(2):
            row = pl.multiple_of(rows[d], S5_BLOCKS)
            new = [None] * (2 * nl)
            for c in range(nl):
                h_re, h_im = h[d * 2 * nl + c], h[d * 2 * nl + nl + c]
                new[c] = a_re[d][c] * h_re - a_im[d][c] * h_im + s_ref[d, c, pl.ds(row, S5_BLOCKS), :]
                new[nl + c] = (a_re[d][c] * h_im + a_im[d][c] * h_re
                               + s_ref[d, nl + c, pl.ds(row, S5_BLOCKS), :])
            for c in range(2 * nl):
                hs_ref[d, c, pl.ds(row, S5_BLOCKS), :] = new[c]
            out += new
        return tuple(out)

    init = tuple(h_ref[d, c] for d in range(2) for c in range(2 * nl))
    h = lax.fori_loop(0, tc, step, init, unroll=4)
    for d in range(2):
        for c in range(2 * nl):
            h_ref[d, c] = h[d * 2 * nl + c]

    for d in range(2):
        for j in range(S5_BLOCKS):
            hj = jnp.concatenate([hs_ref.at[d, c][pl.ds(j, tc, stride=S5_BLOCKS), :] for c in range(2 * nl)],
                                 axis=1).astype(BF16)
            y_refs[d][:, j * gw:(j + 1) * gw] = jnp.dot(hj, wc_ref[d, j], preferred_element_type=F32)


def _s5_scan(hin, u_block, wb, wc, a_re, a_im, tc):
    b, t, _ = hin.shape
    width = S5_GROUPS * S5_GROUP
    half = S5_GROUPS // S5_BLOCKS * S5_STATE
    nchunk = t // tc

    def back(k):
        return jnp.where(k == 0, 0, nchunk - k)

    full = lambda a: pl.BlockSpec(a.shape, lambda bi, k: (0,) * a.ndim)
    out = jax.ShapeDtypeStruct((b, t, width), F32)
    return pl.pallas_call(
        functools.partial(_s5_kernel, tc=tc),
        out_shape=(out, out),
        grid=(b, nchunk),
        in_specs=[pl.BlockSpec((None, tc, width), lambda bi, k: (bi, k, u_block)),
                  pl.BlockSpec((None, tc, width), lambda bi, k: (bi, back(k), u_block)),
                  full(wb), full(wc), full(a_re), full(a_im)],
        out_specs=(pl.BlockSpec((None, tc, width), lambda bi, k: (bi, k, 0)),
                   pl.BlockSpec((None, tc, width), lambda bi, k: (bi, back(k), 0))),
        scratch_shapes=[pltpu.VMEM((2, 2 * half // LANES, tc * S5_BLOCKS, LANES), F32),
                        pltpu.VMEM((2, 2 * half // LANES, tc * S5_BLOCKS, LANES), F32),
                        pltpu.VMEM((2, 2 * half // LANES, S5_BLOCKS, LANES), F32)],
        compiler_params=_params("parallel", "arbitrary"),
    )(hin, hin, wb, wc, a_re, a_im)


def _s5_weights(lam_re, lam_im, log_dt, b_re, b_im, c_re, c_im):
    lr = lam_re.astype(F32)
    li = lam_im.astype(F32)
    dt = jnp.exp(log_dt.astype(F32))[..., None]
    mag = jnp.exp(lr * dt)
    ab_re = mag * jnp.cos(li * dt)
    ab_im = mag * jnp.sin(li * dt)
    den = lr * lr + li * li
    num_re = ab_re - 1.0
    coef_re = ((num_re * lr + ab_im * li) / den)[..., None]
    coef_im = ((ab_im * lr - num_re * li) / den)[..., None]
    br = b_re.astype(F32)
    bi = b_im.astype(F32)
    bb_re = coef_re * br - coef_im * bi
    bb_im = coef_re * bi + coef_im * br
    nb = S5_BLOCKS
    gb = S5_GROUPS // nb
    eye = jnp.eye(gb, dtype=F32)

    def drive(bb):
        w = jnp.einsum('djgpi,gh->djgihp', bb.reshape(2, nb, gb, S5_STATE, S5_GROUP), eye)
        return w.reshape(2, nb, gb * S5_GROUP, gb * S5_STATE)

    def read(c):
        w = jnp.einsum('djgip,gh->djgphi', c.astype(F32).reshape(2, nb, gb, S5_GROUP, S5_STATE), eye)
        return w.reshape(2, nb, gb * S5_STATE, gb * S5_GROUP)

    wb = jnp.concatenate([drive(bb_re), drive(bb_im)], axis=-1).astype(BF16)
    wc = jnp.concatenate([read(c_re), -read(c_im)], axis=-2).astype(BF16)
    return wb, wc, ab_re.reshape(2, nb, gb * S5_STATE), ab_im.reshape(2, nb, gb * S5_STATE)


def _s5_glu_kernel(yf_ref, yb_ref, u_ref, d_ref, w_ref, o_ref):
    y = yf_ref[...] + yb_ref[...] + d_ref[...] * u_ref[...].astype(F32)
    r = jnp.dot(jax.nn.gelu(y).astype(BF16), w_ref[...], preferred_element_type=F32)
    n = r.shape[1] // 2
    o_ref[...] = (r[:, :n] * jax.nn.sigmoid(r[:, n:])).astype(o_ref.dtype)


def _s5_glu(yf, yb, hin, u_block, d_skip, w_glu, tm):
    b, t, width = yf.shape
    return pl.pallas_call(
        _s5_glu_kernel,
        out_shape=jax.ShapeDtypeStruct((b, t, width), BF16),
        grid=(b, t // tm),
        in_specs=[pl.BlockSpec((None, tm, width), lambda bi, i: (bi, i, 0)),
                  pl.BlockSpec((None, tm, width), lambda bi, i: (bi, i, 0)),
                  pl.BlockSpec((None, tm, width), lambda bi, i: (bi, i, u_block)),
                  pl.BlockSpec((1, width), lambda bi, i: (0, 0)),
                  pl.BlockSpec(w_glu.shape, lambda bi, i: (0, 0))],
        out_specs=pl.BlockSpec((None, tm, width), lambda bi, i: (bi, _rotated_tile(i, t // tm), 0)),
        compiler_params=_params("parallel", "parallel"),
    )(yf, yb, hin, d_skip.reshape(1, width).astype(F32), w_glu)


CONV_HALO = 16


def _conv_kernel(ap_ref, gp_ref, ac_ref, gc_ref, an_ref, gn_ref, w_ref, b_ref, lg_ref, lb_ref,
                 o_ref, v_ref, sh_ref, y_ref, *, tr, taps):
    i = pl.program_id(1)
    last = pl.num_programs(1) - 1

    def glu(a_ref, g_ref):
        return a_ref[...].astype(F32) * jax.nn.sigmoid(g_ref[...].astype(F32))

    v_ref[0:CONV_HALO, :] = jnp.where(i > 0, glu(ap_ref, gp_ref), 0.0)
    v_ref[CONV_HALO:CONV_HALO + tr, :] = glu(ac_ref, gc_ref)
    v_ref[CONV_HALO + tr:, :] = jnp.where(i < last, glu(an_ref, gn_ref), 0.0)

    span = v_ref.shape[0] - SUBLANES
    for r in range(1, SUBLANES):
        sh_ref[r - 1, 0:span, :] = v_ref[r:r + span, :]

    first = CONV_HALO - taps // 2
    for c0 in range(0, v_ref.shape[1], LANES):
        acc = jnp.zeros((tr, LANES), F32)
        for k in range(taps):
            tile, r = divmod(first + k, SUBLANES)
            src = v_ref if r == 0 else sh_ref.at[r - 1]
            window = src[tile * SUBLANES:tile * SUBLANES + tr, c0:c0 + LANES]
            acc = acc + w_ref[k:k + 1, c0:c0 + LANES] * window
        y_ref[:, c0:c0 + LANES] = acc

    y = y_ref[...] + b_ref[...]
    yc = y - jnp.mean(y, axis=-1, keepdims=True)
    yn = yc * lax.rsqrt(jnp.mean(yc * yc, axis=-1, keepdims=True) + EPS) * lg_ref[...] + lb_ref[...]
    o_ref[...] = (yn * jax.nn.sigmoid(yn)).astype(o_ref.dtype)


def _conformer_conv(hin, a_block, w_dw, b_dw, g_n, b_n, n_ctx, tr):
    b, t, _ = hin.shape
    n = t - n_ctx
    taps, ch = w_dw.shape
    hb = tr // CONV_HALO
    off = n_ctx // tr
    last_halo = t // CONV_HALO - 1

    def prev_map(cb):
        return lambda bi, i: (bi, (i + off) * hb - 1, cb)

    def next_map(cb):
        return lambda bi, i: (bi, jnp.minimum((i + off + 1) * hb, last_halo), cb)

    def cur_map(cb):
        return lambda bi, i: (bi, i + off, cb)

    vec = lambda v: v.reshape(1, ch).astype(F32)
    halo = lambda m: pl.BlockSpec((None, CONV_HALO, ch), m)
    full = lambda shape: pl.BlockSpec(shape, lambda bi, i: (0, 0))
    return pl.pallas_call(
        functools.partial(_conv_kernel, tr=tr, taps=taps),
        out_shape=jax.ShapeDtypeStruct((b, n, ch), BF16),
        grid=(b, n // tr),
        in_specs=[halo(prev_map(a_block)), halo(prev_map(a_block + 1)),
                  pl.BlockSpec((None, tr, ch), cur_map(a_block)),
                  pl.BlockSpec((None, tr, ch), cur_map(a_block + 1)),
                  halo(next_map(a_block)), halo(next_map(a_block + 1)),
                  full((taps, ch)), full((1, ch)), full((1, ch)), full((1, ch))],
        out_specs=pl.BlockSpec((None, tr, ch), lambda bi, i: (bi, i, 0)),
        scratch_shapes=[pltpu.VMEM((tr + 2 * CONV_HALO, ch), F32),
                        pltpu.VMEM((SUBLANES - 1, tr + 2 * CONV_HALO, ch), F32),
                        pltpu.VMEM((tr, ch), F32)],
        compiler_params=_params("parallel", "parallel"),
    )(hin, hin, hin, hin, hin, hin, w_dw.astype(F32), vec(b_dw), vec(g_n), vec(b_n))


def _ffn_matmuls(h, w1, w3, w2, layer, mod=None):
    b, rows, d = h.shape
    outs = _swiglu(h.reshape(b * rows, d), w1, w3, w2, layer, _row_tile(b * rows, ROW_TILE_CAP), FF_COL_TILE, mod)
    act, w2b = outs[:2]
    y = _matmul([(act, 0, w2b, 0, w2b.shape[0])], _row_tile(b * rows, LONG_K_ROW_TILE_CAP), DOWN_COL_TILE)
    return y.reshape(b, rows, d), (outs[2] if mod is not None else None)


def kernel(x, c, ctx, c_ctx, w_mod, b_mod, g_mix_pre, g_mix_post, g_ffn_pre, g_ffn_post, w_ff1, w_ff3, w_ff2, e_w_in, e_g_q, e_w_uq, e_g_kv, e_w_ukv, e_s5_lam_re, e_s5_lam_im, e_s5_log_dt, e_s5_b_re, e_s5_b_im, e_s5_c_re, e_s5_c_im, e_s5_d, e_w_glu, e_w_out, o_w_in, o_lam_q1, o_lam_k1, o_lam_q2, o_lam_k2, o_g_subln, o_conv_w, o_conv_b, o_conv_norm_g, o_conv_norm_b, o_w_out):
    b, n, d = x.shape
    n_ctx = ctx.shape[1]
    t = n_ctx + n
    depth = w_mod.shape[0]
    assert depth == 2 and e_w_in.shape[0] == 1 and o_w_in.shape[0] == 1
    assert b + 1 <= SUBLANES and n % n_ctx == 0
    tr = n_ctx
    q_rank = e_g_q.shape[1]
    kv_rank = e_g_kv.shape[1]
    s5_w = S5_GROUPS * S5_GROUP

    cvec = jnp.zeros((SUBLANES, d), F32).at[:b].set(c).at[b].set(c_ctx)
    b_mod3 = b_mod.reshape(depth, 1, N_MOD * d)

    def modtab(rows):
        lat = rows[:b].reshape(b, 1, N_MOD, d)
        cx = jnp.broadcast_to(rows[b].reshape(1, 1, N_MOD, d), (b, 1, N_MOD, d))
        return jnp.concatenate([cx, lat], axis=1)

    modtabs = [modtab(_modulation(cvec, w_mod, b_mod3, 1)[0]), None]

    cos, sin = _rope_tables(n, MLA_ROPE)
    rot = _rot_matrix(MLA_ROPE)
    hp = MLA_HEAD_PAD
    cos2 = np.concatenate([cos, cos], axis=1)
    sin2 = np.concatenate([sin, sin], axis=1)
    rot2 = _embed(rot, (LANES, LANES), 0, 0, 0.0)
    rot2[DIFF_QK:, DIFF_QK:] = rot
    pair_tabs = (_embed(cos2, (t, LANES), n_ctx, 0, 1.0), _embed(sin2, (t, LANES), n_ctx, 0, 0.0),
                 jnp.asarray(rot2, BF16))
    q_tab = _embed(np.concatenate([cos, sin], axis=1), (t, hp), n_ctx, MLA_NOPE, 1.0)
    q_tab[:n_ctx, MLA_NOPE + MLA_ROPE:] = 0.0

    lw = e_w_in[0]
    cuts = [q_rank, q_rank + kv_rank, q_rank + kv_rank + MLA_ROPE]
    w_in = jnp.concatenate([lw[:, :cuts[0]], lw[:, cuts[2]:], lw[:, cuts[0]:cuts[1]], lw[:, cuts[1]:cuts[2]],
                            lw[:, cuts[1]:cuts[2]]], axis=1).astype(BF16)
    u_block = q_rank // s5_w
    kv_block = (q_rank + s5_w) // kv_rank
    kr_block = (q_rank + s5_w + kv_rank) // LANES
    w_uq = e_w_uq[0].reshape(q_rank, MLA_HEADS, MLA_NOPE + MLA_ROPE)
    w_uq = jnp.concatenate([w_uq, jnp.einsum('khj,ji->khi', w_uq[:, :, MLA_NOPE:], rot)], axis=2)
    w_uq = w_uq.reshape(q_rank, MLA_HEADS * hp).astype(BF16)
    n_attn = MLA_HEADS * MLA_V

    h = _norm_mod((ctx, x), g_mix_pre[0], modtabs[0], SH_M, SC_M, tr).reshape(b * t, d)
    tm = _row_tile(b * t, ROW_TILE_CAP)
    hin = _matmul([(h, 0, w_in, 0, d)], tm, w_in.shape[1] // 3)
    q = _matmul([(hin, 0, w_uq, 0, q_rank)], tm, UP_COL_TILE, gain=e_g_q[0])
    kv = _matmul([(hin, kv_block, e_w_ukv, 0, kv_rank)], tm, UP_COL_TILE, gain=e_g_kv[0])
    hin3 = hin.reshape(b, t, -1)
    attn = _mla_attention(q.reshape(b, t, -1), kv.reshape(b, t, -1), hin3, kr_block,
                          (q_tab,) + pair_tabs, n_ctx)
    wb, wc, a_re, a_im = _s5_weights(e_s5_lam_re[0], e_s5_lam_im[0], e_s5_log_dt[0], e_s5_b_re[0],
                                     e_s5_b_im[0], e_s5_c_re[0], e_s5_c_im[0])
    yf, yb = _s5_scan(hin3, u_block, wb, wc, a_re, a_im, tr)
    s5 = _s5_glu(yf, yb, hin3, u_block, e_s5_d[0], e_w_glu[0].astype(BF16), tr)
    y = _matmul([(attn.reshape(b * t, -1), 0, e_w_out, 0, n_attn),
                 (s5.reshape(b * t, -1), 0, e_w_out, n_attn // s5_w, s5_w)], tm, PROJ_COL_TILE)
    z, h = _residual((ctx, x), y.reshape(b, t, d), g_mix_post[0], modtabs[0], GT_M, 0, 0, tr,
                     nxt=(g_ffn_pre[0], modtabs[0], SH_F, SC_F), y_rotated=True)
    y, mod1 = _ffn_matmuls(h, w_ff1, w_ff3, w_ff2, 0, mod=(cvec, w_mod, b_mod3, 1))
    modtabs[1] = modtab(mod1)
    z, h = _residual(z, y, g_ffn_post[0], modtabs[0], GT_F, 0, 0, tr,
                     nxt=(g_mix_pre[1], modtabs[1], SH_M, SC_M))

    lam_init = 0.8 - 0.6 * math.exp(-0.3 * 1)
    n_attn = DIFF_HEADS * DIFF_V
    conv_ch = o_conv_w.shape[2]

    hin3 = _matmul([(h.reshape(b * t, d), 0, o_w_in, 0, d)], tm, PROJ_COL_TILE).reshape(b, t, -1)
    lam_vecs = jnp.stack([o_lam_q1[0], o_lam_k1[0], o_lam_q2[0], o_lam_k2[0]]).astype(F32)
    attn = _diff_attention(hin3, pair_tabs, lam_vecs, o_g_subln[0], lam_init, n_ctx)
    conv = _conformer_conv(hin3, 3 * n_attn // conv_ch, o_conv_w[0], o_conv_b[0], o_conv_norm_g[0],
                           o_conv_norm_b[0], n_ctx, tr)
    tml = _row_tile(b * n, ROW_TILE_CAP)
    y = _matmul([(attn.reshape(b * n, -1), 0, o_w_out, 0, n_attn),
                 (conv.reshape(b * n, -1), 0, o_w_out, n_attn // conv_ch, conv_ch)], tml, PROJ_COL_TILE)
    xl, h = _residual(z, y.reshape(b, n, d), g_mix_post[1], modtabs[1], GT_M, n_ctx // tr, 1, tr,
                      nxt=(g_ffn_pre[1], modtabs[1], SH_F, SC_F))
    y, _ = _ffn_matmuls(h, w_ff1, w_ff3, w_ff2, 1)
    return _residual(xl, y, g_ffn_post[1], modtabs[1], GT_F, 0, 1, tr)
```

```python
import functools
import math

import numpy as np

import jax
import jax.numpy as jnp
from jax import lax
from jax.experimental import pallas as pl
from jax.experimental.pallas import tpu as pltpu

F32 = jnp.float32
BF16 = jnp.bfloat16

EPS = 1e-6
ROPE_BASE = 10000.0
GRID_W = 64
LOG2E = math.log2(math.e)

MLA_HEADS = 16
MLA_NOPE = 128
MLA_ROPE = 64
MLA_V = 128
MLA_HEAD_PAD = 256
MLA_HEADS_PER_STEP = 2
DIFF_HEADS = 16
DIFF_QK = 64
DIFF_V = 128
S5_GROUP = 16
S5_GROUPS = 64
S5_STATE = 64
S5_BLOCKS = 8
N_MOD = 6

LANES = 128
SUBLANES = 8
VMEM_LIMIT = 56 * 1024 * 1024

ROW_TILE_CAP = 1100
LONG_K_ROW_TILE_CAP = 550
FF_COL_TILE = 256
DOWN_COL_TILE = 512
PROJ_COL_TILE = 512
UP_COL_TILE = 2048
MOD_COL_TILE = 512

SH_M, SC_M, GT_M, SH_F, SC_F, GT_F = range(N_MOD)


def _params(*sem):
    return pltpu.CompilerParams(dimension_semantics=sem, vmem_limit_bytes=VMEM_LIMIT)


def _row_tile(rows, cap):
    k = -(-rows // cap)
    while rows % k or (rows // k) % 16:
        k += 1
    return rows // k


def _mod_rows(c_ref, w_ref, b_ref):
    c = c_ref[...]
    a = (c * jax.nn.sigmoid(c)).astype(BF16)
    return jnp.dot(a, w_ref[...].astype(BF16), preferred_element_type=F32) + b_ref[...]


def _mod_kernel(c_ref, w_ref, b_ref, o_ref):
    o_ref[...] = _mod_rows(c_ref, w_ref, b_ref)


def _modulation(cvec, w_mod, b_mod, layers):
    _, d, n = w_mod.shape
    tn = MOD_COL_TILE
    return pl.pallas_call(
        _mod_kernel,
        out_shape=jax.ShapeDtypeStruct((layers, SUBLANES, n), F32),
        grid=(layers, n // tn),
        in_specs=[pl.BlockSpec((SUBLANES, d), lambda l, j: (0, 0)),
                  pl.BlockSpec((None, d, tn), lambda l, j: (l, 0, j)),
                  pl.BlockSpec((None, 1, tn), lambda l, j: (l, 0, j))],
        out_specs=pl.BlockSpec((None, SUBLANES, tn), lambda l, j: (l, 0, j)),
        compiler_params=_params("parallel", "parallel"),
    )(cvec, w_mod, b_mod)


def _stream_specs(src, tr, d, tile_offset):
    if isinstance(src, tuple):
        return ([pl.BlockSpec((None, tr, d), lambda bi, i: (bi, 0, 0)),
                 pl.BlockSpec((None, tr, d), lambda bi, i: (bi, jnp.maximum(i - 1, 0), 0))], list(src))
    return [pl.BlockSpec((None, tr, d), lambda bi, i: (bi, i + tile_offset, 0))], [src]


def _stream_rows(x_refs):
    if len(x_refs) == 1:
        return x_refs[0][...]
    return jnp.where(pl.program_id(1) == 0, x_refs[0][...], x_refs[1][...])


def _mod_spec(d, seg_offset):
    return pl.BlockSpec((None, None, N_MOD, d), lambda bi, i: (bi, jnp.minimum(i + seg_offset, 1), 0, 0))


def _adaln(x, g_ref, m_ref, shift_row, scale_row):
    r = lax.rsqrt(jnp.mean(x * x, axis=-1, keepdims=True) + EPS)
    h = x * r * g_ref[...]
    return (h * (1.0 + m_ref[scale_row:scale_row + 1, :]) + m_ref[shift_row:shift_row + 1, :]).astype(BF16)


def _norm_mod_kernel(*refs, n_src, shift_row, scale_row):
    g_ref, m_ref, o_ref = refs[n_src:]
    o_ref[...] = _adaln(_stream_rows(refs[:n_src]), g_ref, m_ref, shift_row, scale_row)


def _norm_mod(src, gain, modtab, shift_row, scale_row, tr):
    specs, args = _stream_specs(src, tr, gain.shape[0], 0)
    b = args[0].shape[0]
    d = gain.shape[0]
    rows = sum(a.shape[1] for a in args)
    return pl.pallas_call(
        functools.partial(_norm_mod_kernel, n_src=len(args), shift_row=shift_row, scale_row=scale_row),
        out_shape=jax.ShapeDtypeStruct((b, rows, d), BF16),
        grid=(b, rows // tr),
        in_specs=specs + [pl.BlockSpec((1, d), lambda bi, i: (0, 0)), _mod_spec(d, 0)],
        out_specs=pl.BlockSpec((None, tr, d), lambda bi, i: (bi, i, 0)),
        compiler_params=_params("parallel", "parallel"),
    )(*args, gain.reshape(1, d), modtab)


def _residual_kernel(*refs, n_src, gate_row, nxt_rows):
    y_ref, g_ref, m_ref = refs[n_src:n_src + 3]
    y = y_ref[...].astype(F32)
    r = lax.rsqrt(jnp.mean(y * y, axis=-1, keepdims=True) + EPS)
    z = _stream_rows(refs[:n_src]) + m_ref[gate_row:gate_row + 1, :] * (y * r * g_ref[...])
    if nxt_rows is None:
        refs[-1][...] = z
    else:
        g2_ref, m2_ref, o_ref, h_ref = refs[n_src + 3:]
        o_ref[...] = z
        h_ref[...] = _adaln(z, g2_ref, m2_ref, *nxt_rows)


def _residual(src, y, gain, modtab, gate_row, x_tile_offset, seg_offset, tr, nxt=None, y_rotated=False):
    b, rows, d = y.shape
    specs, args = _stream_specs(src, tr, d, x_tile_offset)
    tile = lambda: pl.BlockSpec((None, tr, d), lambda bi, i: (bi, i, 0))
    vec = lambda: pl.BlockSpec((1, d), lambda bi, i: (0, 0))
    y_spec = tile()
    if y_rotated:
        y_spec = pl.BlockSpec((None, tr, d), lambda bi, i: (bi, _rotated_tile(i, rows // tr), 0))
    in_specs = specs + [y_spec, vec(), _mod_spec(d, seg_offset)]
    args = args + [y, gain.reshape(1, d), modtab]
    out_shape = jax.ShapeDtypeStruct((b, rows, d), F32)
    out_specs = tile()
    nxt_rows = None
    if nxt is not None:
        in_specs += [vec(), _mod_spec(d, seg_offset)]
        args += [nxt[0].reshape(1, d), nxt[1]]
        nxt_rows = (nxt[2], nxt[3])
        out_shape = (out_shape, jax.ShapeDtypeStruct((b, rows, d), BF16))
        out_specs = (out_specs, tile())
    return pl.pallas_call(
        functools.partial(_residual_kernel, n_src=len(specs), gate_row=gate_row, nxt_rows=nxt_rows),
        out_shape=out_shape,
        grid=(b, rows // tr),
        in_specs=in_specs,
        out_specs=out_specs,
        compiler_params=_params("parallel", "parallel"),
    )(*args)


def _bf16(w):
    return w if w.dtype == BF16 else w.astype(BF16)


def _matmul_kernel(*refs, n_pairs, normed):
    o_ref = refs[-1]
    acc = None
    for p in range(n_pairs):
        a = refs[2 * p][...]
        if normed:
            af = a.astype(F32)
            r = lax.rsqrt(jnp.mean(af * af, axis=-1, keepdims=True) + EPS)
            a = (af * r * refs[2 * n_pairs][...]).astype(BF16)
        part = jnp.dot(a, _bf16(refs[2 * p + 1][...]), preferred_element_type=F32)
        acc = part if acc is None else acc + part
    o_ref[...] = acc.astype(o_ref.dtype)


def _weight_spec(w, k, row_block, layer, tn):
    if w.ndim == 3:
        return pl.BlockSpec((None, k, tn), lambda i, j: (layer, row_block, j))
    return pl.BlockSpec((k, tn), lambda i, j: (row_block, j))


def _matmul(pairs, tm, tn, gain=None, layer=0, out_dtype=BF16):
    rows = pairs[0][0].shape[0]
    n = pairs[0][2].shape[-1]
    tn = min(tn, n)
    in_specs, args = [], []
    for a, cb, w, rb, k in pairs:
        in_specs.append(pl.BlockSpec((tm, k), functools.partial(lambda i, j, cb: (i, cb), cb=cb)))
        in_specs.append(_weight_spec(w, k, rb, layer, tn))
        args += [a, w]
    if gain is not None:
        k = pairs[0][4]
        in_specs.append(pl.BlockSpec((1, k), lambda i, j: (0, 0)))
        args.append(gain.reshape(1, k).astype(F32))
    return pl.pallas_call(
        functools.partial(_matmul_kernel, n_pairs=len(pairs), normed=gain is not None),
        out_shape=jax.ShapeDtypeStruct((rows, n), out_dtype),
        grid=(rows // tm, n // tn),
        in_specs=in_specs,
        out_specs=pl.BlockSpec((tm, tn), lambda i, j: (i, j)),
        compiler_params=_params("parallel", "arbitrary"),
    )(*args)


RING_DEPTH = 3


def _ring_matmul_kernel(a_ref, w_hbm, o_ref, wbuf, sems, *, nj, tn, layer):
    step = pl.program_id(0) * nj + pl.program_id(1)
    total = pl.num_programs(0) * nj

    def copy(s):
        slot = s % RING_DEPTH
        return pltpu.make_async_copy(w_hbm.at[layer, :, pl.ds((s % nj) * tn, tn)], wbuf.at[slot], sems.at[slot])

    @pl.when(step == 0)
    def _():
        for s in range(RING_DEPTH - 1):
            copy(s).start()

    @pl.when(step + RING_DEPTH - 1 < total)
    def _():
        copy(step + RING_DEPTH - 1).start()

    copy(step).wait()
    o_ref[...] = jnp.dot(a_ref[...], wbuf[step % RING_DEPTH].astype(BF16),
                         preferred_element_type=F32).astype(o_ref.dtype)


def _ring_matmul(a, w, layer, tm, tn):
    rows, k = a.shape
    n = w.shape[2]
    nj = n // tn
    assert (rows // tm) * nj >= RING_DEPTH
    return pl.pallas_call(
        functools.partial(_ring_matmul_kernel, nj=nj, tn=tn, layer=layer),
        out_shape=jax.ShapeDtypeStruct((rows, n), BF16),
        grid=(rows // tm, nj),
        in_specs=[pl.BlockSpec((tm, k), lambda i, j: (i, 0)), pl.BlockSpec(memory_space=pl.ANY)],
        out_specs=pl.BlockSpec((tm, tn), lambda i, j: (i, j)),
        scratch_shapes=[pltpu.VMEM((RING_DEPTH, k, tn), w.dtype), pltpu.SemaphoreType.DMA((RING_DEPTH,))],
        compiler_params=_params("arbitrary", "arbitrary"),
    )(a, w)


def _swiglu_kernel(*refs, mod_tiles):
    a_ref, w1_ref, w3_ref, w2_ref = refs[:4]
    o_ref, w2b_ref = refs[-2 - (mod_tiles > 0):][:2]
    a = a_ref[...]
    g = jnp.dot(a, _bf16(w1_ref[...]), preferred_element_type=F32)
    u = jnp.dot(a, _bf16(w3_ref[...]), preferred_element_type=F32)
    o_ref[...] = (g * jax.nn.sigmoid(g) * u).astype(o_ref.dtype)

    @pl.when(pl.program_id(0) == 0)
    def _():
        w2b_ref[...] = w2_ref[...].astype(BF16)

    if mod_tiles:
        c_ref, wm_ref, bm_ref = refs[4:7]

        @pl.when(pl.program_id(0) * pl.num_programs(1) + pl.program_id(1) < mod_tiles)
        def _():
            refs[-1][...] = _mod_rows(c_ref, wm_ref, bm_ref)


def _swiglu(a, w1, w3, w2, layer, tm, tn, mod=None):
    rows, k = a.shape
    f = w1.shape[2]
    d = w2.shape[2]
    ni, nj = rows // tm, f // tn

    def slab(i, j):
        return jnp.where(i == 0, j, nj - 1)

    in_specs = [pl.BlockSpec((tm, k), lambda i, j: (i, 0)),
                _weight_spec(w1, k, 0, layer, tn), _weight_spec(w3, k, 0, layer, tn),
                pl.BlockSpec((None, tn, d), lambda i, j: (layer, slab(i, j), 0))]
    out_shape = [jax.ShapeDtypeStruct((rows, f), BF16), jax.ShapeDtypeStruct((f, d), BF16)]
    out_specs = [pl.BlockSpec((tm, tn), lambda i, j: (i, j)),
                 pl.BlockSpec((tn, d), lambda i, j: (slab(i, j), 0))]
    args = [a, w1, w3, w2]
    mod_tiles = 0
    if mod is not None:
        cvec, w_mod, b_mod, ml = mod
        n_mod = w_mod.shape[2]
        per_step = -(-n_mod // (LANES * ni * nj))
        while (n_mod // LANES) % per_step:
            per_step += 1
        mod_tile = per_step * LANES
        mod_tiles = n_mod // mod_tile

        def tile(i, j):
            return jnp.minimum(i * nj + j, mod_tiles - 1)

        in_specs += [pl.BlockSpec(cvec.shape, lambda i, j: (0, 0)),
                     pl.BlockSpec((None, w_mod.shape[1], mod_tile), lambda i, j: (ml, 0, tile(i, j))),
                     pl.BlockSpec((None, 1, mod_tile), lambda i, j: (ml, 0, tile(i, j)))]
        out_shape.append(jax.ShapeDtypeStruct((SUBLANES, n_mod), F32))
        out_specs.append(pl.BlockSpec((SUBLANES, mod_tile), lambda i, j: (0, tile(i, j))))
        args += [cvec, w_mod, b_mod]
    return pl.pallas_call(
        functools.partial(_swiglu_kernel, mod_tiles=mod_tiles),
        out_shape=tuple(out_shape),
        grid=(ni, nj),
        in_specs=in_specs,
        out_specs=tuple(out_specs),
        compiler_params=_params("arbitrary", "arbitrary"),
    )(*args)


def _rope_tables(n_tok, dim):
    rows = n_tok // GRID_W
    row = np.repeat(np.arange(rows, dtype=np.float32), GRID_W)
    col = np.tile(np.arange(GRID_W, dtype=np.float32), rows)
    axis_dim = dim // 2
    expo = -np.arange(0, axis_dim, 2, dtype=np.float32) / np.float32(axis_dim)
    inv = np.power(np.float32(ROPE_BASE), expo).astype(np.float32)
    ang_r = row[:, None] * inv[None, :]
    ang_c = col[:, None] * inv[None, :]
    ang = np.concatenate([ang_r, ang_r, ang_c, ang_c], axis=-1).astype(np.float32)
    return np.cos(ang), np.sin(ang)


def _rot_matrix(dim):
    q = dim // 4
    p = np.zeros((dim, dim), np.float32)
    eye = np.eye(q, dtype=np.float32)
    p[q:2 * q, 0:q] = -eye
    p[0:q, q:2 * q] = eye
    p[3 * q:, 2 * q:3 * q] = -eye
    p[2 * q:3 * q, 3 * q:] = eye
    return p


def _signed(sin, quarter):
    lane = np.arange(sin.shape[1])
    return np.where(lane % (2 * quarter) < quarter, -sin, sin).astype(np.float32)


def _embed(block, shape, r0, c0, fill):
    out = np.full(shape, fill, np.float32)
    out[r0:r0 + block.shape[0], c0:c0 + block.shape[1]] = block
    return out


def _rope(x_bf16, cos, sin, rot):
    turned = jnp.dot(x_bf16, rot, preferred_element_type=F32)
    return x_bf16.astype(F32) * cos + turned * sin


def _rope_lanes(x_bf16, cos, sin_signed, quarter):
    x = x_bf16.astype(F32)
    lane = lax.broadcasted_iota(jnp.int32, x.shape, 1)
    right = pltpu.roll(x, x.shape[1] - quarter, 1)
    left = pltpu.roll(x, quarter, 1)
    turned = jnp.where(lane % (2 * quarter) < quarter, right, left)
    return x * cos + turned * sin_signed


KEY_CHUNK = 1024


def _softmax_values(scores, vones):
    keys = scores[0].shape[1]
    maxes = [jnp.max(s, axis=-1, keepdims=True) for s in scores]
    accs = [None] * len(scores)
    for c0 in range(0, keys, KEY_CHUNK):
        c1 = min(c0 + KEY_CHUNK, keys)
        for i, (s, m, vone) in enumerate(zip(scores, maxes, vones)):
            e = jnp.exp2(s[:, c0:c1] - m).astype(BF16)
            part = jnp.dot(e, vone[c0:c1, :], preferred_element_type=F32)
            accs[i] = part if accs[i] is None else accs[i] + part
    dv = accs[0].shape[1] // 2
    return [a[:, :dv] / a[:, dv:] for a in accs]


def _const_spec(shape):
    return pl.BlockSpec(shape, lambda bi, h, i: (0,) * len(shape), pipeline_mode=pl.Buffered(1))


_NT = (((1,), (1,)), ((), ()))


def _mla_kernel(*refs, tiles, n_ctx, scale, heads):
    q_refs, t_refs = refs[:tiles], refs[tiles:2 * tiles]
    kv_ref, kr_ref, ck_ref, sk_ref, pk_ref, o_ref, kcat_ref, vone_ref = refs[2 * tiles:]
    step = pl.program_id(2)
    hp = MLA_HEAD_PAD
    tq = q_refs[0].shape[0]

    @pl.when(step == 0)
    def _():
        kr = _rope(kr_ref[...], ck_ref[...], sk_ref[...], pk_ref[...]).astype(BF16)
        ones = jnp.ones((kr.shape[0], hp - MLA_V), BF16)
        for h in range(heads):
            kcat_ref[h, :, :MLA_NOPE] = kv_ref[:, h * hp:h * hp + MLA_NOPE]
            kcat_ref[h, :, MLA_NOPE:] = kr
            vone_ref[h, :, :MLA_V] = kv_ref[:, h * hp + MLA_NOPE:(h + 1) * hp]
            vone_ref[h, :, MLA_V:] = ones

    def attend(n_tiles, rows):
        scores = []
        for r in range(n_tiles):
            for h in range(heads):
                q = q_refs[r][:, h * hp:(h + 1) * hp].astype(F32) * t_refs[r][...] * scale
                scores.append(lax.dot_general(q.astype(BF16), kcat_ref[h, :rows, :], _NT,
                                              preferred_element_type=F32))
        outs = _softmax_values(scores, [vone_ref.at[h, :rows, :] for _ in range(n_tiles) for h in range(heads)])
        for r in range(n_tiles):
            for h in range(heads):
                o_ref[r * tq:(r + 1) * tq, h * MLA_V:(h + 1) * MLA_V] = outs[r * heads + h].astype(o_ref.dtype)

    @pl.when(step == 0)
    def _():
        attend(1, n_ctx)

    @pl.when(step > 0)
    def _():
        attend(tiles, kcat_ref.shape[1])


MLA_TILES_PER_STEP = 2


def _mla_attention(q, kv, hin, kr_block, tabs, n_ctx):
    b, t, _ = q.shape
    tq = n_ctx
    tq_tab, ck, sk, pk = tabs
    scale = (MLA_NOPE + MLA_ROPE) ** -0.5 * LOG2E
    hp = MLA_HEAD_PAD
    hs = MLA_HEADS_PER_STEP
    tiles = MLA_TILES_PER_STEP
    n_lat = (t - n_ctx) // tq
    assert hp == 2 * MLA_V and hp == MLA_NOPE + 2 * MLA_ROPE == MLA_NOPE + LANES and n_lat % tiles == 0

    def q_tile(s, r):
        return jnp.where(s == 0, r, tiles * (s - 1) + 1 + r)

    def out_block(s):
        return jnp.where(s == 0, n_lat // tiles, s - 1)

    q_specs = [pl.BlockSpec((None, tq, hs * hp), functools.partial(lambda bi, h, s, r: (bi, q_tile(s, r), h), r=r))
               for r in range(tiles)]
    t_specs = [pl.BlockSpec((tq, hp), functools.partial(lambda bi, h, s, r: (q_tile(s, r), 0), r=r))
               for r in range(tiles)]
    return pl.pallas_call(
        functools.partial(_mla_kernel, tiles=tiles, n_ctx=n_ctx, scale=scale, heads=hs),
        out_shape=jax.ShapeDtypeStruct((b, t, MLA_HEADS * MLA_V), BF16),
        grid=(b, MLA_HEADS // hs, 1 + n_lat // tiles),
        in_specs=q_specs + t_specs + [
            pl.BlockSpec((None, t, hs * hp), lambda bi, h, s: (bi, 0, h)),
            pl.BlockSpec((None, t, LANES), lambda bi, h, s: (bi, 0, kr_block)),
            _const_spec((t, LANES)), _const_spec((t, LANES)), _const_spec((LANES, LANES))],
        out_specs=pl.BlockSpec((None, tiles * tq, hs * MLA_V), lambda bi, h, s: (bi, out_block(s), h)),
        scratch_shapes=[pltpu.VMEM((hs, t, hp), BF16), pltpu.VMEM((hs, t, hp), BF16)],
        compiler_params=_params("parallel", "parallel", "arbitrary"),
    )(*([q] * tiles), *([tq_tab] * tiles), kv, hin, ck, sk, pk)


def _rotated_tile(i, n_tiles):
    return jnp.where(i == 0, n_tiles - 1, i - 1)


DIFF_TILES_PER_STEP = 2


def _diff_kernel(*refs, tiles, scale, lam_init):
    q_refs = refs[:tiles]
    k_ref, v_ref, cq_ref, sq_ref, ck_ref, sk_ref, p_ref, lam_ref, g_ref, o_ref, kro_ref, vone_ref = refs[tiles:]
    tq = q_refs[0].shape[0]

    @pl.when(pl.program_id(2) == 0)
    def _():
        kro_ref[...] = _rope(k_ref[...], ck_ref[...], sk_ref[...], p_ref[...]).astype(BF16)
        vone_ref[:, :DIFF_V] = v_ref[...]
        vone_ref[:, DIFF_V:] = jnp.ones((v_ref.shape[0], DIFF_V), BF16)

    lv = lam_ref[...]
    lam = (jnp.exp(jnp.sum(lv[0:1] * lv[1:2], axis=-1, keepdims=True))
           - jnp.exp(jnp.sum(lv[2:3] * lv[3:4], axis=-1, keepdims=True)) + lam_init)

    def scores(qm):
        return lax.dot_general(qm.astype(BF16), kro_ref[...], _NT, preferred_element_type=F32)

    chains = []
    for r, q_ref in enumerate(q_refs):
        rows = slice(r * tq, (r + 1) * tq)
        q = _rope_lanes(q_ref[...], cq_ref[rows, :], sq_ref[rows, :], DIFF_QK // 4) * scale
        lane = lax.broadcasted_iota(jnp.int32, q.shape, 1)
        chains += [scores(jnp.where(lane < DIFF_QK, q, 0.0)), scores(jnp.where(lane >= DIFF_QK, q, 0.0))]
    outs = _softmax_values(chains, [vone_ref] * len(chains))
    for r in range(tiles):
        o = outs[2 * r] - lam * outs[2 * r + 1]
        norm = lax.rsqrt(jnp.mean(o * o, axis=-1, keepdims=True) + EPS)
        o_ref[r * tq:(r + 1) * tq, :] = (o * norm * g_ref[...] * (1.0 - lam_init)).astype(o_ref.dtype)


def _diff_attention(hin, tabs, lam_vecs, g_subln, lam_init, n_ctx):
    b, t, _ = hin.shape
    tq = n_ctx
    n = t - n_ctx
    cos, sin, p = tabs
    h_all = DIFF_HEADS
    tiles = DIFF_TILES_PER_STEP
    off = n_ctx // tq
    q_specs = [pl.BlockSpec((None, tq, LANES), functools.partial(lambda bi, h, i, r: (bi, tiles * i + off + r, h), r=r))
               for r in range(tiles)]
    q_tab = pl.BlockSpec((tiles * tq, LANES), lambda bi, h, i: (i, 0))
    return pl.pallas_call(
        functools.partial(_diff_kernel, tiles=tiles, scale=DIFF_QK ** -0.5 * LOG2E, lam_init=lam_init),
        out_shape=jax.ShapeDtypeStruct((b, n, h_all * DIFF_V), BF16),
        grid=(b, h_all, n // (tiles * tq)),
        in_specs=q_specs + [
            pl.BlockSpec((None, t, LANES), lambda bi, h, i: (bi, 0, h_all + h)),
            pl.BlockSpec((None, t, LANES), lambda bi, h, i: (bi, 0, 2 * h_all + h)),
            q_tab, q_tab,
            _const_spec((t, LANES)), _const_spec((t, LANES)),
            _const_spec((LANES, LANES)), _const_spec((4, DIFF_QK)), _const_spec((1, DIFF_V))],
        out_specs=pl.BlockSpec((None, tiles * tq, DIFF_V), lambda bi, h, i: (bi, i, h)),
        scratch_shapes=[pltpu.VMEM((t, LANES), BF16), pltpu.VMEM((t, 2 * DIFF_V), BF16)],
        compiler_params=_params("parallel", "parallel", "arbitrary"),
    )(*([hin] * tiles), hin, hin, cos[n_ctx:], _signed(sin[n_ctx:], DIFF_QK // 4), cos, sin, p, lam_vecs,
      g_subln.reshape(1, DIFF_V).astype(F32))


def _s5_kernel(uf_ref, ub_ref, wb_ref, wc_ref, are_ref, aim_ref, yf_ref, yb_ref, s_ref, hs_ref, h_ref,
               *, tc):
    half = S5_GROUPS // S5_BLOCKS * S5_STATE
    gw = S5_GROUPS // S5_BLOCKS * S5_GROUP
    nl = half // LANES
    u_refs = (uf_ref, ub_ref)
    y_refs = (yf_ref, yb_ref)

    @pl.when(pl.program_id(1) == 0)
    def _():
        h_ref[...] = jnp.zeros_like(h_ref)

    for d in range(2):
        for j in range(S5_BLOCKS):
            bu = jnp.dot(u_refs[d][:, j * gw:(j + 1) * gw], wb_ref[d, j], preferred_element_type=F32)
            for c in range(2 * nl):
                s_ref.at[d, c][pl.ds(j, tc, stride=S5_BLOCKS), :] = bu[:, c * LANES:(c + 1) * LANES]

    a_re = [[are_ref[d, :, c * LANES:(c + 1) * LANES] for c in range(nl)] for d in range(2)]
    a_im = [[aim_ref[d, :, c * LANES:(c + 1) * LANES] for c in range(nl)] for d in range(2)]

    def step(t, h):
        rows = (t * S5_BLOCKS, (tc - 1 - t) * S5_BLOCKS)
        out = []
        for d in range(2):
            row = pl.multiple_of(rows[d], S5_BLOCKS)
            new = [None] * (2 * nl)
            for c in range(nl):
                h_re, h_im = h[d * 2 * nl + c], h[d * 2 * nl + nl + c]
                new[c] = a_re[d][c] * h_re - a_im[d][c] * h_im + s_ref[d, c, pl.ds(row, S5_BLOCKS), :]
                new[nl + c] = (a_re[d][c] * h_im + a_im[d][c] * h_re
                               + s_ref[d, nl + c, pl.ds(row, S5_BLOCKS), :])
            for c in range(2 * nl):
                hs_ref[d, c, pl.ds(row, S5_BLOCKS), :] = new[c]
            out += new
        return tuple(out)

    init = tuple(h_ref[d, c] for d in range(2) for c in range(2 * nl))
    h = lax.fori_loop(0, tc, step, init, unroll=4)
    for d in range(2):
        for c in range(2 * nl):
            h_ref[d, c] = h[d * 2 * nl + c]

    for d in range(2):
        for j in range(S5_BLOCKS):
            hj = jnp.concatenate([hs_ref.at[d, c][pl.ds(j, tc, stride=S5_BLOCKS), :] for c in range(2 * nl)],
                                 axis=1).astype(BF16)
            y_refs[d][:, j * gw:(j + 1) * gw] = jnp.dot(hj, wc_ref[d, j], preferred_element_type=F32)


def _s5_scan(hin, u_block, wb, wc, a_re, a_im, tc):
    b, t, _ = hin.shape
    width = S5_GROUPS * S5_GROUP
    half = S5_GROUPS // S5_BLOCKS * S5_STATE
    nchunk = t // tc

    def back(k):
        return jnp.where(k == 0, 0, nchunk - k)

    full = lambda a: pl.BlockSpec(a.shape, lambda bi, k: (0,) * a.ndim)
    out = jax.ShapeDtypeStruct((b, t, width), F32)
    return pl.pallas_call(
        functools.partial(_s5_kernel, tc=tc),
        out_shape=(out, out),
        grid=(b, nchunk),
        in_specs=[pl.BlockSpec((None, tc, width), lambda bi, k: (bi, k, u_block)),
                  pl.BlockSpec((None, tc, width), lambda bi, k: (bi, back(k), u_block)),
                  full(wb), full(wc), full(a_re), full(a_im)],
        out_specs=(pl.BlockSpec((None, tc, width), lambda bi, k: (bi, k, 0)),
                   pl.BlockSpec((None, tc, width), lambda bi, k: (bi, back(k), 0))),
        scratch_shapes=[pltpu.VMEM((2, 2 * half // LANES, tc * S5_BLOCKS, LANES), F32),
                        pltpu.VMEM((2, 2 * half // LANES, tc * S5_BLOCKS, LANES), F32),
                        pltpu.VMEM((2, 2 * half // LANES, S5_BLOCKS, LANES), F32)],
        compiler_params=_params("parallel", "arbitrary"),
    )(hin, hin, wb, wc, a_re, a_im)


def _s5_weights(lam_re, lam_im, log_dt, b_re, b_im, c_re, c_im):
    lr = lam_re.astype(F32)
    li = lam_im.astype(F32)
    dt = jnp.exp(log_dt.astype(F32))[..., None]
    mag = jnp.exp(lr * dt)
    ab_re = mag * jnp.cos(li * dt)
    ab_im = mag * jnp.sin(li * dt)
    den = lr * lr + li * li
    num_re = ab_re - 1.0
    coef_re = ((num_re * lr + ab_im * li) / den)[..., None]
    coef_im = ((ab_im * lr - num_re * li) / den)[..., None]
    br = b_re.astype(F32)
    bi = b_im.astype(F32)
    bb_re = coef_re * br - coef_im * bi
    bb_im = coef_re * bi + coef_im * br
    nb = S5_BLOCKS
    gb = S5_GROUPS // nb
    eye = jnp.eye(gb, dtype=F32)

    def drive(bb):
        w = jnp.einsum('djgpi,gh->djgihp', bb.reshape(2, nb, gb, S5_STATE, S5_GROUP), eye)
        return w.reshape(2, nb, gb * S5_GROUP, gb * S5_STATE)

    def read(c):
        w = jnp.einsum('djgip,gh->djgphi', c.astype(F32).reshape(2, nb, gb, S5_GROUP, S5_STATE), eye)
        return w.reshape(2, nb, gb * S5_STATE, gb * S5_GROUP)

    wb = jnp.concatenate([drive(bb_re), drive(bb_im)], axis=-1).astype(BF16)
    wc = jnp.concatenate([read(c_re), -read(c_im)], axis=-2).astype(BF16)
    return wb, wc, ab_re.reshape(2, nb, gb * S5_STATE), ab_im.reshape(2, nb, gb * S5_STATE)


def _s5_glu_kernel(yf_ref, yb_ref, u_ref, d_ref, w_ref, o_ref):
    y = yf_ref[...] + yb_ref[...] + d_ref[...] * u_ref[...].astype(F32)
    r = jnp.dot(jax.nn.gelu(y).astype(BF16), w_ref[...], preferred_element_type=F32)
    n = r.shape[1] // 2
    o_ref[...] = (r[:, :n] * jax.nn.sigmoid(r[:, n:])).astype(o_ref.dtype)


def _s5_glu(yf, yb, hin, u_block, d_skip, w_glu, tm):
    b, t, width = yf.shape
    return pl.pallas_call(
        _s5_glu_kernel,
        out_shape=jax.ShapeDtypeStruct((b, t, width), BF16),
        grid=(b, t // tm),
        in_specs=[pl.BlockSpec((None, tm, width), lambda bi, i: (bi, i, 0)),
                  pl.BlockSpec((None, tm, width), lambda bi, i: (bi, i, 0)),
                  pl.BlockSpec((None, tm, width), lambda bi, i: (bi, i, u_block)),
                  pl.BlockSpec((1, width), lambda bi, i: (0, 0)),
                  pl.BlockSpec(w_glu.shape, lambda bi, i: (0, 0))],
        out_specs=pl.BlockSpec((None, tm, width), lambda bi, i: (bi, _rotated_tile(i, t // tm), 0)),
        compiler_params=_params("parallel", "parallel"),
    )(yf, yb, hin, d_skip.reshape(1, width).astype(F32), w_glu)


CONV_HALO = 16


def _conv_kernel(ap_ref, gp_ref, ac_ref, gc_ref, an_ref, gn_ref, w_ref, b_ref, lg_ref, lb_ref,
                 o_ref, v_ref, sh_ref, y_ref, *, tr, taps):
    i = pl.program_id(1)
    last = pl.num_programs(1) - 1

    def glu(a_ref, g_ref):
        return a_ref[...].astype(F32) * jax.nn.sigmoid(g_ref[...].astype(F32))

    v_ref[0:CONV_HALO, :] = jnp.where(i > 0, glu(ap_ref, gp_ref), 0.0)
    v_ref[CONV_HALO:CONV_HALO + tr, :] = glu(ac_ref, gc_ref)
    v_ref[CONV_HALO + tr:, :] = jnp.where(i < last, glu(an_ref, gn_ref), 0.0)

    span = v_ref.shape[0] - SUBLANES
    for r in range(1, SUBLANES):
        sh_ref[r - 1, 0:span, :] = v_ref[r:r + span, :]

    first = CONV_HALO - taps // 2
    for c0 in range(0, v_ref.shape[1], LANES):
        acc = jnp.zeros((tr, LANES), F32)
        for k in range(taps):
            tile, r = divmod(first + k, SUBLANES)
            src = v_ref if r == 0 else sh_ref.at[r - 1]
            window = src[tile * SUBLANES:tile * SUBLANES + tr, c0:c0 + LANES]
            acc = acc + w_ref[k:k + 1, c0:c0 + LANES] * window
        y_ref[:, c0:c0 + LANES] = acc

    y = y_ref[...] + b_ref[...]
    yc = y - jnp.mean(y, axis=-1, keepdims=True)
    yn = yc * lax.rsqrt(jnp.mean(yc * yc, axis=-1, keepdims=True) + EPS) * lg_ref[...] + lb_ref[...]
    o_ref[...] = (yn * jax.nn.sigmoid(yn)).astype(o_ref.dtype)


def _conformer_conv(hin, a_block, w_dw, b_dw, g_n, b_n, n_ctx, tr):
    b, t, _ = hin.shape
    n = t - n_ctx
    taps, ch = w_dw.shape
    hb = tr // CONV_HALO
    off = n_ctx // tr
    last_halo = t // CONV_HALO - 1

    def prev_map(cb):
        return lambda bi, i: (bi, (i + off) * hb - 1, cb)

    def next_map(cb):
        return lambda bi, i: (bi, jnp.minimum((i + off + 1) * hb, last_halo), cb)

    def cur_map(cb):
        return lambda bi, i: (bi, i + off, cb)

    vec = lambda v: v.reshape(1, ch).astype(F32)
    halo = lambda m: pl.BlockSpec((None, CONV_HALO, ch), m)
    full = lambda shape: pl.BlockSpec(shape, lambda bi, i: (0, 0))
    return pl.pallas_call(
        functools.partial(_conv_kernel, tr=tr, taps=taps),
        out_shape=jax.ShapeDtypeStruct((b, n, ch), BF16),
        grid=(b, n // tr),
        in_specs=[halo(prev_map(a_block)), halo(prev_map(a_block + 1)),
                  pl.BlockSpec((None, tr, ch), cur_map(a_block)),
                  pl.BlockSpec((None, tr, ch), cur_map(a_block + 1)),
                  halo(next_map(a_block)), halo(next_map(a_block + 1)),
                  full((taps, ch)), full((1, ch)), full((1, ch)), full((1, ch))],
        out_specs=pl.BlockSpec((None, tr, ch), lambda bi, i: (bi, i, 0)),
        scratch_shapes=[pltpu.VMEM((tr + 2 * CONV_HALO, ch), F32),
                        pltpu.VMEM((SUBLANES - 1, tr + 2 * CONV_HALO, ch), F32),
                        pltpu.VMEM((tr, ch), F32)],
        compiler_params=_params("parallel", "parallel"),
    )(hin, hin, hin, hin, hin, hin, w_dw.astype(F32), vec(b_dw), vec(g_n), vec(b_n))


def _ffn_matmuls(h, w1, w3, w2, layer, mod=None):
    b, rows, d = h.shape
    outs = _swiglu(h.reshape(b * rows, d), w1, w3, w2, layer, _row_tile(b * rows, ROW_TILE_CAP), FF_COL_TILE, mod)
    act, w2b = outs[:2]
    y = _matmul([(act, 0, w2b, 0, w2b.shape[0])], _row_tile(b * rows, LONG_K_ROW_TILE_CAP), DOWN_COL_TILE)
    return y.reshape(b, rows, d), (outs[2] if mod is not None else None)


def kernel(x, c, ctx, c_ctx, w_mod, b_mod, g_mix_pre, g_mix_post, g_ffn_pre, g_ffn_post, w_ff1, w_ff3, w_ff2, e_w_in, e_g_q, e_w_uq, e_g_kv, e_w_ukv, e_s5_lam_re, e_s5_lam_im, e_s5_log_dt, e_s5_b_re, e_s5_b_im, e_s5_c_re, e_s5_c_im, e_s5_d, e_w_glu, e_w_out, o_w_in, o_lam_q1, o_lam_k1, o_lam_q2, o_lam_k2, o_g_subln, o_conv_w, o_conv_b, o_conv_norm_g, o_conv_norm_b, o_w_out):
    b, n, d = x.shape
    n_ctx = ctx.shape[1]
    t = n_ctx + n
    depth = w_mod.shape[0]
    assert depth == 2 and e_w_in.shape[0] == 1 and o_w_in.shape[0] == 1
    assert b + 1 <= SUBLANES and n % n_ctx == 0
    tr = n_ctx
    q_rank = e_g_q.shape[1]
    kv_rank = e_g_kv.shape[1]
    s5_w = S5_GROUPS * S5_GROUP

    cvec = jnp.zeros((SUBLANES, d), F32).at[:b].set(c).at[b].set(c_ctx)
    b_mod3 = b_mod.reshape(depth, 1, N_MOD * d)

    def modtab(rows):
        lat = rows[:b].reshape(b, 1, N_MOD, d)
        cx = jnp.broadcast_to(rows[b].reshape(1, 1, N_MOD, d), (b, 1, N_MOD, d))
        return jnp.concatenate([cx, lat], axis=1)

    modtabs = [modtab(_modulation(cvec, w_mod, b_mod3, 1)[0]), None]

    cos, sin = _rope_tables(n, MLA_ROPE)
    rot = _rot_matrix(MLA_ROPE)
    hp = MLA_HEAD_PAD
    cos2 = np.concatenate([cos, cos], axis=1)
    sin2 = np.concatenate([sin, sin], axis=1)
    rot2 = _embed(rot, (LANES, LANES), 0, 0, 0.0)
    rot2[DIFF_QK:, DIFF_QK:] = rot
    pair_tabs = (_embed(cos2, (t, LANES), n_ctx, 0, 1.0), _embed(sin2, (t, LANES), n_ctx, 0, 0.0),
                 jnp.asarray(rot2, BF16))
    q_tab = _embed(np.concatenate([cos, sin], axis=1), (t, hp), n_ctx, MLA_NOPE, 1.0)
    q_tab[:n_ctx, MLA_NOPE + MLA_ROPE:] = 0.0

    lw = e_w_in[0]
    cuts = [q_rank, q_rank + kv_rank, q_rank + kv_rank + MLA_ROPE]
    w_in = jnp.concatenate([lw[:, :cuts[0]], lw[:, cuts[2]:], lw[:, cuts[0]:cuts[1]], lw[:, cuts[1]:cuts[2]],
                            lw[:, cuts[1]:cuts[2]]], axis=1).astype(BF16)
    u_block = q_rank // s5_w
    kv_block = (q_rank + s5_w) // kv_rank
    kr_block = (q_rank + s5_w + kv_rank) // LANES
    w_uq = e_w_uq[0].reshape(q_rank, MLA_HEADS, MLA_NOPE + MLA_ROPE)
    w_uq = jnp.concatenate([w_uq, jnp.einsum('khj,ji->khi', w_uq[:, :, MLA_NOPE:], rot)], axis=2)
    w_uq = w_uq.reshape(q_rank, MLA_HEADS * hp).astype(BF16)
    n_attn = MLA_HEADS * MLA_V

    h = _norm_mod((ctx, x), g_mix_pre[0], modtabs[0], SH_M, SC_M, tr).reshape(b * t, d)
    tm = _row_tile(b * t, ROW_TILE_CAP)
    hin = _matmul([(h, 0, w_in, 0, d)], tm, w_in.shape[1] // 3)
    q = _matmul([(hin, 0, w_uq, 0, q_rank)], tm, UP_COL_TILE, gain=e_g_q[0])
    kv = _matmul([(hin, kv_block, e_w_ukv, 0, kv_rank)], tm, UP_COL_TILE, gain=e_g_kv[0])
    hin3 = hin.reshape(b, t, -1)
    attn = _mla_attention(q.reshape(b, t, -1), kv.reshape(b, t, -1), hin3, kr_block,
                          (q_tab,) + pair_tabs, n_ctx)
    wb, wc, a_re, a_im = _s5_weights(e_s5_lam_re[0], e_s5_lam_im[0], e_s5_log_dt[0], e_s5_b_re[0],
                                     e_s5_b_im[0], e_s5_c_re[0], e_s5_c_im[0])
    yf, yb = _s5_scan(hin3, u_block, wb, wc, a_re, a_im, tr)
    s5 = _s5_glu(yf, yb, hin3, u_block, e_s5_d[0], e_w_glu[0].astype(BF16), tr)
    y = _matmul([(attn.reshape(b * t, -1), 0, e_w_out, 0, n_attn),
                 (s5.reshape(b * t, -1), 0, e_w_out, n_attn // s5_w, s5_w)], tm, PROJ_COL_TILE)
    z, h = _residual((ctx, x), y.reshape(b, t, d), g_mix_post[0], modtabs[0], GT_M, 0, 0, tr,
                     nxt=(g_ffn_pre[0], modtabs[0], SH_F, SC_F), y_rotated=True)
    y, mod1 = _ffn_matmuls(h, w_ff1, w_ff3, w_ff2, 0, mod=(cvec, w_mod, b_mod3, 1))
    modtabs[1] = modtab(mod1)
    z, h = _residual(z, y, g_ffn_post[0], modtabs[0], GT_F, 0, 0, tr,
                     nxt=(g_mix_pre[1], modtabs[1], SH_M, SC_M))

    lam_init = 0.8 - 0.6 * math.exp(-0.3 * 1)
    n_attn = DIFF_HEADS * DIFF_V
    conv_ch = o_conv_w.shape[2]

    hin3 = _ring_matmul(h.reshape(b * t, d), o_w_in, 0, tm, PROJ_COL_TILE).reshape(b, t, -1)
    lam_vecs = jnp.stack([o_lam_q1[0], o_lam_k1[0], o_lam_q2[0], o_lam_k2[0]]).astype(F32)
    attn = _diff_attention(hin3, pair_tabs, lam_vecs, o_g_subln[0], lam_init, n_ctx)
    conv = _conformer_conv(hin3, 3 * n_attn // conv_ch, o_conv_w[0], o_conv_b[0], o_conv_norm_g[0],
                           o_conv_norm_b[0], n_ctx, tr)
    tml = _row_tile(b * n, ROW_TILE_CAP)
    y = _matmul([(attn.reshape(b * n, -1), 0, o_w_out, 0, n_attn),
                 (conv.reshape(b * n, -1), 0, o_w_out, n_attn // conv_ch, conv_ch)], tml, PROJ_COL_TILE)
    xl, h = _residual(z, y.reshape(b, n, d), g_mix_post[1], modtabs[1], GT_M, n_ctx // tr, 1, tr,
                      nxt=(g_ffn_pre[1], modtabs[1], SH_F, SC_F))
    y, _ = _ffn_matmuls(h, w_ff1, w_ff3, w_ff2, 1)
    return _residual(xl, y, g_ffn_post[1], modtabs[1], GT_F, 0, 1, tr)
```
